```python
import jax, jax.numpy as jnp
from jax import lax
import numpy as np

D_MODEL = 1024
BATCH = 8
SEQ = 2048
DEPTH = 1
DEC_BATCH = 128
DEC_SEQ = 1
PAST_LEN = 16384
PAGE_SIZE = 128

MIX_W = D_MODEL
POOL_W = MIX_W // 2
RWKV_W = MIX_W - POOL_W
POOL_WINDOWS = (2, 4, 8, 16)
N_POOL_GROUPS = len(POOL_WINDOWS)
POOL_GC = POOL_W // N_POOL_GROUPS
POOL_BUF = max(POOL_WINDOWS) - 1
HEAD_DIM = 64
N_HEADS = RWKV_W // HEAD_DIM
W_LORA = 64
A_LORA = 64
G_LORA = 128
SHIFT_W = 3 * RWKV_W + W_LORA + A_LORA + G_LORA
IN_W = POOL_W + SHIFT_W
D_FF = 4 * D_MODEL
RMS_EPS = 1e-6
GN_EPS = 64e-5
NORM_EPS = 1e-12

kernel_name = 'hymba_pool_rwkv7_decode_step'


def _rmsnorm(x, g):
    xf = x.astype(jnp.float32)
    y = xf * lax.rsqrt(jnp.mean(xf * xf, axis=-1, keepdims=True) + RMS_EPS)
    return (y * g.astype(jnp.float32)).astype(x.dtype)


def _pool_mixer(u, buf, pos0, pool_w, pool_scale):
    B, T, C = u.shape
    f32 = jnp.float32
    ext = jnp.concatenate([buf.astype(f32), u.astype(f32)], axis=1)
    cs = jnp.concatenate([jnp.zeros((B, 1, C), f32), jnp.cumsum(ext, axis=1)], axis=1)
    end = cs[:, POOL_BUF + 1:]
    pos = pos0 + jnp.arange(T, dtype=jnp.int32)
    means = []
    for gi, win in enumerate(POOL_WINDOWS):
        sl = slice(gi * POOL_GC, (gi + 1) * POOL_GC)
        start = cs[:, POOL_BUF + 1 - win: POOL_BUF + 1 - win + T, sl]
        cnt = jnp.minimum(win, pos + 1).astype(f32)[None, :, None]
        means.append((end[..., sl] - start) / cnt)
    mean = jnp.concatenate(means, axis=-1)
    d = (mean - ext[:, POOL_BUF:]).reshape(B, T, N_POOL_GROUPS, POOL_GC)
    out = jnp.einsum('btgc,gcd->btgd', d, pool_w.astype(f32)).reshape(B, T, C) * pool_scale.astype(f32)
    new_buf = ext[:, -POOL_BUF:].astype(buf.dtype)
    return out, new_buf


def _rwkv7_mixer(p, shift_prev, S0, shift_mu, w0, w_lora_up, a0, a_lora_up, g_lora_up,
                 k_k, k_a, r_k, ln_x_g, ln_x_b):
    B, T, _ = p.shape
    f32 = jnp.float32
    pf = p.astype(f32)
    prev = jnp.concatenate([shift_prev.astype(f32), pf[:, :-1]], axis=1)
    ps = pf + (prev - pf) * shift_mu.astype(f32)
    cuts = [RWKV_W, 2 * RWKV_W, 3 * RWKV_W, 3 * RWKV_W + W_LORA, 3 * RWKV_W + W_LORA + A_LORA]
    r, k, v, zw, za, zg = jnp.split(ps, cuts, axis=-1)
    w_log = -jax.nn.softplus(-(w0 + jnp.tanh(zw) @ w_lora_up)) - 0.5
    decay = jnp.exp(-jnp.exp(w_log))
    a = jax.nn.sigmoid(a0 + za @ a_lora_up)
    g = jax.nn.sigmoid(zg) @ g_lora_up
    heads = lambda t: t.reshape(B, T, N_HEADS, HEAD_DIM)
    kk = heads(k * k_k)
    kk = kk * lax.rsqrt(jnp.sum(kk * kk, axis=-1, keepdims=True) + NORM_EPS)
    k = k * (1.0 + (a - 1.0) * k_a)
    r, k, v, decay, a = map(heads, (r, k, v, decay, a))

    def step(S, inp):
        r_t, w_t, k_t, v_t, aa_t, bb_t = inp
        sa = jnp.einsum('bhvk,bhk->bhv', S, aa_t)
        S = S * w_t[:, :, None, :] + sa[..., None] * bb_t[:, :, None, :] + v_t[..., None] * k_t[:, :, None, :]
        return S, jnp.einsum('bhvk,bhk->bhv', S, r_t)

    xs = tuple(jnp.moveaxis(t, 1, 0) for t in (r, decay, k, v, -kk, kk * a))
    S_final, y = lax.scan(step, S0.astype(f32), xs)
    y = jnp.moveaxis(y, 0, 1)
    mu = jnp.mean(y, axis=-1, keepdims=True)
    var = jnp.mean(jnp.square(y - mu), axis=-1, keepdims=True)
    y = ((y - mu) * lax.rsqrt(var + GN_EPS)).reshape(B, T, RWKV_W) * ln_x_g + ln_x_b
    bonus = jnp.sum(r * k * r_k, axis=-1, keepdims=True) * v
    y = (y + bonus.reshape(B, T, RWKV_W)) * g
    return y, S_final.astype(S0.dtype), p[:, -1:]


def _layer(x, S0, shift_prev, pool_buf, pos0, norm1_g, w_in, shift_mu, pool_w, pool_scale,
           w0, w_lora_up, a0, a_lora_up, g_lora_up, k_k, k_a, r_k, ln_x_g, ln_x_b,
           w_out, norm2_g, w_up, w_down):
    h = _rmsnorm(x, norm1_g)
    proj = h @ w_in
    pool_out, new_pool = _pool_mixer(proj[..., :POOL_W], pool_buf, pos0, pool_w, pool_scale)
    rwkv_out, new_S, new_shift = _rwkv7_mixer(proj[..., POOL_W:], shift_prev, S0, shift_mu, w0,
                                              w_lora_up, a0, a_lora_up, g_lora_up, k_k, k_a, r_k,
                                              ln_x_g, ln_x_b)
    mix = jnp.concatenate([pool_out, rwkv_out], axis=-1).astype(x.dtype) @ w_out
    x = x + mix.astype(x.dtype)
    h2 = _rmsnorm(x, norm2_g)
    x = x + (jnp.square(jax.nn.relu(h2 @ w_up)) @ w_down).astype(x.dtype)
    return x, new_S, new_shift, new_pool


def setup_inputs(seed: int = 0) -> dict:
    key = jax.random.key(seed)
    ks = jax.random.split(key, 32)
    f32 = jnp.float32
    nrm = lambda k, shape, s: jax.random.normal(k, shape, f32) * s
    L = DEPTH
    return {
        'x_prompt': nrm(ks[0], (BATCH, SEQ, D_MODEL), 1.0),
        'x_sample': nrm(ks[1], (DEC_BATCH, DEC_SEQ, D_MODEL), 1.0),
        'state_wkv': nrm(ks[2], (L, DEC_BATCH, N_HEADS, HEAD_DIM, HEAD_DIM), 0.1),
        'state_shift': nrm(ks[3], (L, DEC_BATCH, 1, SHIFT_W), 1.0),
        'state_pool': nrm(ks[4], (L, DEC_BATCH, POOL_BUF, POOL_W), 1.0),
        'norm1_g': 1.0 + nrm(ks[5], (L, D_MODEL), 0.02),
        'w_in': nrm(ks[6], (L, D_MODEL, IN_W), D_MODEL ** -0.5),
        'shift_mu': jax.random.uniform(ks[7], (L, SHIFT_W), f32, 0.0, 1.0),
        'pool_w': nrm(ks[8], (L, N_POOL_GROUPS, POOL_GC, POOL_GC), POOL_GC ** -0.5),
        'pool_scale': 1.0 + nrm(ks[9], (L, POOL_W), 0.1),
        'w0': -1.0 + nrm(ks[10], (L, RWKV_W), 0.5),
        'w_lora_up': nrm(ks[11], (L, W_LORA, RWKV_W), 0.5 * W_LORA ** -0.5),
        'a0': nrm(ks[12], (L, RWKV_W), 0.3),
        'a_lora_up': nrm(ks[13], (L, A_LORA, RWKV_W), 0.5 * A_LORA ** -0.5),
        'g_lora_up': nrm(ks[14], (L, G_LORA, RWKV_W), G_LORA ** -0.5),
        'k_k': 0.85 + nrm(ks[15], (L, RWKV_W), 0.05),
        'k_a': 1.0 + nrm(ks[16], (L, RWKV_W), 0.05),
        'r_k': nrm(ks[17], (L, N_HEADS, HEAD_DIM), 0.1),
        'ln_x_g': 1.0 + nrm(ks[18], (L, RWKV_W), 0.02),
        'ln_x_b': nrm(ks[19], (L, RWKV_W), 0.02),
        'w_out': nrm(ks[20], (L, MIX_W, D_MODEL), MIX_W ** -0.5),
        'norm2_g': 1.0 + nrm(ks[21], (L, D_MODEL), 0.02),
        'w_up': nrm(ks[22], (L, D_MODEL, D_FF), D_MODEL ** -0.5),
        'w_down': nrm(ks[23], (L, D_FF, D_MODEL), D_FF ** -0.5),
        'norm_f_g': 1.0 + nrm(ks[24], (D_MODEL,), 0.02),
    }


def reference(x_prompt, x_sample, state_wkv, state_shift, state_pool, norm1_g, w_in, shift_mu,
              pool_w, pool_scale, w0, w_lora_up, a0, a_lora_up, g_lora_up, k_k, k_a, r_k,
              ln_x_g, ln_x_b, w_out, norm2_g, w_up, w_down, norm_f_g):
    dt_p = x_prompt.dtype
    zero_wkv = jnp.zeros((BATCH, N_HEADS, HEAD_DIM, HEAD_DIM), state_wkv.dtype)
    zero_shift = jnp.zeros((BATCH, 1, SHIFT_W), dt_p)
    zero_pool = jnp.zeros((BATCH, POOL_BUF, POOL_W), dt_p)
    hp, hs = x_prompt, x_sample
    wkv_p, sh_p, pb_p, wkv_s, sh_s, pb_s = [], [], [], [], [], []
    for l in range(DEPTH):
        lw = (norm1_g[l], w_in[l], shift_mu[l], pool_w[l], pool_scale[l], w0[l], w_lora_up[l],
              a0[l], a_lora_up[l], g_lora_up[l], k_k[l], k_a[l], r_k[l], ln_x_g[l], ln_x_b[l],
              w_out[l], norm2_g[l], w_up[l], w_down[l])
        hp, s1, s2, s3 = _layer(hp, zero_wkv, zero_shift, zero_pool, 0, *lw)
        hs, t1, t2, t3 = _layer(hs, state_wkv[l], state_shift[l], state_pool[l], PAST_LEN, *lw)
        wkv_p.append(s1); sh_p.append(s2); pb_p.append(s3)
        wkv_s.append(t1); sh_s.append(t2); pb_s.append(t3)
    y_prompt = _rmsnorm(hp, norm_f_g)
    y_sample = _rmsnorm(hs, norm_f_g)
    return (y_prompt, y_sample, jnp.stack(wkv_p), jnp.stack(sh_p), jnp.stack(pb_p),
            jnp.stack(wkv_s), jnp.stack(sh_s), jnp.stack(pb_s))
```

```python
import functools

import jax
import jax.numpy as jnp
import numpy as np
from jax import lax
from jax.experimental import pallas as pl
from jax.experimental.pallas import tpu as pltpu

F32 = jnp.float32
BF16 = jnp.bfloat16

D_MODEL = 1024
POOL_W = 512
RWKV_W = 512
POOL_WINDOWS = (2, 4, 8, 16)
POOL_GC = 128
POOL_BUF = 15
HEAD_DIM = 64
N_HEADS = 8
SHIFT_W = 1792
IN_W = POOL_W + SHIFT_W
D_FF = 4096
LORA_Z0 = 3 * RWKV_W
GATE_Z0 = LORA_Z0 + 128
RMS_EPS = 1e-6
GN_EPS = 64e-5
NORM_EPS = 1e-12

CHUNK = 64
GROUP_HEADS = 4
GROUP_W = GROUP_HEADS * HEAD_DIM
N_GROUPS = N_HEADS // GROUP_HEADS
POOL_HALO = 16
ROW_BLOCK = 512
FF_BLOCK = 1024
DEC_BLOCK = 8
VMEM_LIMIT = 56 * 1024 * 1024


def _dot(a, b):
    return jnp.dot(a, b, preferred_element_type=F32)


def _dot_nt(a, b):
    return lax.dot_general(a, b, (((1,), (1,)), ((), ())), preferred_element_type=F32)


def _rmsnorm(x, g):
    return x * lax.rsqrt(jnp.mean(x * x, axis=-1, keepdims=True) + RMS_EPS) * g


def _sigmoid(x):
    return 1.0 / (1.0 + jnp.exp(-x))


def _softplus(x):
    return jnp.maximum(x, 0.0) + jnp.log(1.0 + jnp.exp(-jnp.abs(x)))


def _head_sum(x, bones):
    parts = [_dot(x[:, g * GROUP_W:(g + 1) * GROUP_W].astype(BF16), bones) for g in range(N_GROUPS)]
    return jnp.concatenate(parts, axis=1)


def _rwkv_prelude(pf, prev, mu, w0, a0, k_k, k_a, w2, gl_up, bones):
    ps = pf + (prev - pf) * mu
    r = ps[:, 0:RWKV_W]
    k = ps[:, RWKV_W:2 * RWKV_W]
    v = ps[:, 2 * RWKV_W:3 * RWKV_W]
    z = ps[:, LORA_Z0:GATE_Z0]
    lane = lax.broadcasted_iota(jnp.int32, z.shape, 1)
    zt = jnp.where(lane < 64, jnp.tanh(z), z)
    wa = _dot(zt.astype(BF16), w2)
    w_log = -_softplus(-(w0 + wa[:, :RWKV_W])) - 0.5
    logw = -jnp.exp(w_log)
    a = _sigmoid(a0 + wa[:, RWKV_W:])
    g = _dot(_sigmoid(ps[:, GATE_Z0:SHIFT_W]).astype(BF16), gl_up)
    kk = k * k_k
    kk = kk * lax.rsqrt(_head_sum(kk * kk, bones) + NORM_EPS)
    k2 = k * (1.0 + (a - 1.0) * k_a)
    return r, k2, v, logw, a, g, kk


def _rwkv_postlude(y, r, k2, v, g, r_k, ln_g, ln_b, bones):
    mean = _head_sum(y, bones) * (1.0 / HEAD_DIM)
    yc = y - mean
    var = _head_sum(yc * yc, bones) * (1.0 / HEAD_DIM)
    yn = yc * lax.rsqrt(var + GN_EPS) * ln_g + ln_b
    bonus = _head_sum(r * k2 * r_k, bones) * v
    return (yn + bonus) * g


def _inproj_pool_kernel(x_ref, g1_ref, win_ref, pw_ref, ps_ref, u_ref, p_ref, po_ref, ext_ref):
    tb = pl.program_id(1)
    rows = x_ref.shape[0]

    @pl.when(tb == 0)
    def _():
        ext_ref[0:POOL_HALO, :] = jnp.zeros((POOL_HALO, POOL_W), F32)

    h = _rmsnorm(x_ref[...], g1_ref[...]).astype(BF16)
    proj = _dot(h, win_ref[...])
    u = proj[:, :POOL_W]
    u_ref[...] = u
    p_ref[...] = proj[:, POOL_W:]
    ext_ref[POOL_HALO:POOL_HALO + rows, :] = u

    pos = tb * rows + lax.broadcasted_iota(jnp.int32, (rows, 1), 0)
    outs = []
    for gi, win in enumerate(POOL_WINDOWS):
        cols = slice(gi * POOL_GC, (gi + 1) * POOL_GC)
        acc = u[:, cols]
        for j in range(1, win):
            acc = acc + ext_ref[POOL_HALO - j:POOL_HALO - j + rows, cols]
        cnt = jnp.minimum(win, pos + 1).astype(F32)
        d = acc / cnt - u[:, cols]
        outs.append(_dot(d.astype(BF16), pw_ref[gi]))
    po_ref[...] = (jnp.concatenate(outs, axis=1) * ps_ref[...]).astype(po_ref.dtype)
    ext_ref[0:POOL_HALO, :] = ext_ref[rows:rows + POOL_HALO, :]


def _inproj_pool(x, g1, win, pw, ps):
    b, t, _ = x.shape
    rows = ROW_BLOCK
    const = lambda *shape: pl.BlockSpec(shape, lambda i, j: (0,) * len(shape))
    return pl.pallas_call(
        _inproj_pool_kernel,
        grid=(b, t // rows),
        in_specs=[
            pl.BlockSpec((None, rows, D_MODEL), lambda i, j: (i, j, 0)),
            const(1, D_MODEL),
            const(D_MODEL, IN_W),
            const(len(POOL_WINDOWS), POOL_GC, POOL_GC),
            const(1, POOL_W),
        ],
        out_specs=[
            pl.BlockSpec((None, rows, POOL_W), lambda i, j: (i, j, 0)),
            pl.BlockSpec((None, rows, SHIFT_W), lambda i, j: (i, j, 0)),
            pl.BlockSpec((None, rows, POOL_W), lambda i, j: (i, j, 0)),
        ],
        out_shape=[
            jax.ShapeDtypeStruct((b, t, POOL_W), F32),
            jax.ShapeDtypeStruct((b, t, SHIFT_W), F32),
            jax.ShapeDtypeStruct((b, t, POOL_W), BF16),
        ],
        scratch_shapes=[pltpu.VMEM((rows + POOL_HALO, POOL_W), F32)],
        compiler_params=pltpu.CompilerParams(
            dimension_semantics=("arbitrary", "arbitrary"), vmem_limit_bytes=VMEM_LIMIT),
        name="inproj_pool",
    )(x, g1, win, pw, ps)


def _expand(x_bf, hm):
    return jnp.concatenate([x_bf] * GROUP_HEADS, axis=0) * hm


def _chunk_group(s0, r, k2, v, aa, bb, w_in, w_ex, w_inv, strict, incl, iden, hm):
    c = r.shape[0]
    at = (aa * w_ex).astype(BF16)
    rt = r * w_in
    bt = (bb * w_inv).astype(BF16)
    kt = (k2 * w_inv).astype(BF16)
    e_b = _expand(bt, hm)
    e_k = _expand(kt, hm)
    e_v = _expand(v.astype(BF16), hm)
    e_a = _expand(at, hm)
    e_bk = jnp.concatenate([e_b, e_k], axis=0)
    sc = _dot_nt(jnp.concatenate([at, rt.astype(BF16)], axis=0), e_bk)
    l_ab = sc[:c, :GROUP_W] * strict
    l_ak = sc[:c, GROUP_W:] * strict
    m_rb = sc[c:, :GROUP_W] * incl
    m_rk = sc[c:, GROUP_W:] * incl

    blockdiag = lambda x: _expand(x.astype(BF16), hm)
    n = iden + l_ab
    p = _dot(l_ab.astype(BF16), blockdiag(l_ab))
    levels = int(np.log2(c))
    for _ in range(levels - 2):
        out = _dot(jnp.concatenate([p, n], axis=0).astype(BF16), blockdiag(p))
        p = out[:c]
        n = n + out[c:]
    t_inv = n + _dot(n.astype(BF16), blockdiag(p))

    l_akv = _dot(l_ak.astype(BF16), e_v)
    tau = _dot(t_inv.astype(BF16), jnp.concatenate([e_a, blockdiag(l_akv)], axis=1))
    e_ta = blockdiag(tau[:, :GROUP_W])
    e_u0 = blockdiag(tau[:, GROUP_W:])
    m1 = _dot(m_rb.astype(BF16), jnp.concatenate([e_ta, e_u0], axis=1))
    q = rt + m1[:, :GROUP_W]
    y0 = m1[:, GROUP_W:] + _dot(m_rk.astype(BF16), e_v)
    tr = _dot_nt(iden.astype(BF16), jnp.concatenate([e_ta, e_u0, e_v], axis=0))
    g_t = _dot(tr[:, :GROUP_W].astype(BF16), e_b)
    h_t = _dot(tr[:, GROUP_W:].astype(BF16), e_bk)
    s_bf = s0.astype(BF16)
    y = _dot_nt(q.astype(BF16), _expand(s_bf, hm)) + y0
    s_new = (s0 + _dot(s_bf, blockdiag(g_t)) + h_t) * w_in[c - 1:c, :]
    return y, s_new


def _rwkv_chunk_kernel(p_ref, mu_ref, w0_ref, a0_ref, kk_ref, ka_ref, rk_ref, lng_ref, lnb_ref,
                       w2_ref, glup_ref, bones_ref, strict_ref, incl_ref, iden_ref, hm_ref, tri_ref,
                       rw_ref, sfin_ref, s_scr, prev_scr):
    t = pl.program_id(1)
    c = p_ref.shape[0]

    @pl.when(t == 0)
    def _():
        s_scr[...] = jnp.zeros_like(s_scr)
        prev_scr[...] = jnp.zeros_like(prev_scr)

    pf = p_ref[...]
    row = lax.broadcasted_iota(jnp.int32, (c, 1), 0)
    prev = jnp.where(row == 0, prev_scr[0:1, :], pltpu.roll(pf, 1, 0))
    bones = bones_ref[...]
    r, k2, v, logw, a, g, kk = _rwkv_prelude(
        pf, prev, mu_ref[...], w0_ref[...], a0_ref[...], kk_ref[...], ka_ref[...],
        w2_ref[...], glup_ref[...], bones)
    aa = -kk
    bb = kk * a

    l1 = logw.astype(BF16)
    r1 = logw - l1.astype(F32)
    l2 = r1.astype(BF16)
    l3 = (r1 - l2.astype(F32)).astype(BF16)
    cl = _dot(tri_ref[...], jnp.concatenate([l1, l2, l3, jnp.zeros_like(l1)], axis=0))
    w_in = jnp.exp(cl)
    w_ex = jnp.exp(cl - logw)
    w_inv = jnp.exp(-cl)

    strict = strict_ref[...]
    incl = incl_ref[...]
    iden = iden_ref[...]
    hm = hm_ref[...]
    ys = []
    for gi in range(N_GROUPS):
        sl = slice(gi * GROUP_W, (gi + 1) * GROUP_W)
        y_g, s_g = _chunk_group(s_scr[:, sl], r[:, sl], k2[:, sl], v[:, sl], aa[:, sl], bb[:, sl],
                                w_in[:, sl], w_ex[:, sl], w_inv[:, sl], strict, incl, iden, hm)
        s_scr[:, sl] = s_g
        ys.append(y_g)
    y = jnp.concatenate(ys, axis=1)
    out = _rwkv_postlude(y, r, k2, v, g, rk_ref[...], lng_ref[...], lnb_ref[...], bones)
    rw_ref[...] = out.astype(rw_ref.dtype)
    prev_scr[0:1, :] = pf[c - 1:c, :]

    @pl.when(t == pl.num_programs(1) - 1)
    def _():
        sfin_ref[...] = s_scr[...]


def _chunk_constants():
    c = CHUNK
    t = np.arange(c)[:, None]
    s = (np.arange(GROUP_HEADS * c) % c)[None, :]
    strict = (s < t).astype(np.float32)
    incl = (s <= t).astype(np.float32)
    iden = (s == t).astype(np.float32)
    row_h = np.arange(GROUP_HEADS * c)[:, None] // c
    col_h = np.arange(GROUP_W)[None, :] // HEAD_DIM
    hm = (row_h == col_h).astype(np.float32)
    tri = (np.arange(c)[None, :] <= np.arange(c)[:, None]).astype(np.float32)
    tri = np.concatenate([tri, tri, tri, np.zeros_like(tri)], axis=1)
    return (jnp.asarray(strict), jnp.asarray(incl), jnp.asarray(iden),
            jnp.asarray(hm, dtype=BF16), jnp.asarray(tri, dtype=BF16))


def _head_ones():
    h = np.arange(GROUP_W) // HEAD_DIM
    return jnp.asarray((h[:, None] == h[None, :]).astype(np.float32), dtype=BF16)


def _rwkv_chunked(p, rows_prm, w2, gl_up, bones):
    b, t, _ = p.shape
    c = CHUNK
    consts = _chunk_constants()
    const2 = lambda arr: pl.BlockSpec(arr.shape, lambda i, j: (0, 0))
    operands = list(rows_prm) + [w2, gl_up, bones] + list(consts)
    return pl.pallas_call(
        _rwkv_chunk_kernel,
        grid=(b, t // c),
        in_specs=[pl.BlockSpec((None, c, SHIFT_W), lambda i, j: (i, j, 0))] + [const2(a) for a in operands],
        out_specs=[
            pl.BlockSpec((None, c, RWKV_W), lambda i, j: (i, j, 0)),
            pl.BlockSpec((None, HEAD_DIM, RWKV_W), lambda i, j: (i, 0, 0)),
        ],
        out_shape=[
            jax.ShapeDtypeStruct((b, t, RWKV_W), BF16),
            jax.ShapeDtypeStruct((b, HEAD_DIM, RWKV_W), F32),
        ],
        scratch_shapes=[pltpu.VMEM((HEAD_DIM, RWKV_W), F32), pltpu.VMEM((8, SHIFT_W), F32)],
        compiler_params=pltpu.CompilerParams(
            dimension_semantics=("arbitrary", "arbitrary"), vmem_limit_bytes=VMEM_LIMIT),
        name="rwkv_chunked",
    )(p, *operands)


def _outproj_mlp_kernel(x_ref, po_ref, rw_ref, wout_ref, g2_ref, wup_ref, wdn_ref, gf_ref, y_ref):
    mix = _dot(po_ref[...], wout_ref[0:POOL_W, :]) + _dot(rw_ref[...], wout_ref[POOL_W:, :])
    x1 = x_ref[...] + mix
    h2 = _rmsnorm(x1, g2_ref[...]).astype(BF16)
    acc = jnp.zeros_like(x1)
    for j in range(D_FF // FF_BLOCK):
        cols = slice(j * FF_BLOCK, (j + 1) * FF_BLOCK)
        hid = jnp.maximum(_dot(h2, wup_ref[:, cols]), 0.0)
        acc = acc + _dot((hid * hid).astype(BF16), wdn_ref[cols, :])
    y_ref[...] = _rmsnorm(x1 + acc, gf_ref[...])


def _outproj_mlp(x, po, rw, wout, g2, wup, wdn, gf):
    n = x.shape[0]
    rows = min(ROW_BLOCK, n)
    const = lambda *shape: pl.BlockSpec(shape, lambda i: (0,) * len(shape))
    return pl.pallas_call(
        _outproj_mlp_kernel,
        grid=(n // rows,),
        in_specs=[
            pl.BlockSpec((rows, D_MODEL), lambda i: (i, 0)),
            pl.BlockSpec((rows, POOL_W), lambda i: (i, 0)),
            pl.BlockSpec((rows, RWKV_W), lambda i: (i, 0)),
            const(D_MODEL, D_MODEL),
            const(1, D_MODEL),
            const(D_MODEL, D_FF),
            const(D_FF, D_MODEL),
            const(1, D_MODEL),
        ],
        out_specs=pl.BlockSpec((rows, D_MODEL), lambda i: (i, 0)),
        out_shape=jax.ShapeDtypeStruct((n, D_MODEL), F32),
        compiler_params=pltpu.CompilerParams(
            dimension_semantics=("arbitrary",), vmem_limit_bytes=VMEM_LIMIT),
        name="outproj_mlp",
    )(x, po, rw, wout, g2, wup, wdn, gf)


def _sample_front_kernel(x_ref, sp_ref, sh_ref, g1_ref, win_ref, pw_ref, ps_ref, mu_ref, w0_ref, a0_ref,
                         kk_ref, ka_ref, w2_ref, glup_ref, bones_ref,
                         u_ref, p_ref, po_ref, vec_ref, gate_ref):
    h = _rmsnorm(x_ref[...], g1_ref[...]).astype(BF16)
    proj = _dot(h, win_ref[...])
    u = proj[:, :POOL_W]
    pf = proj[:, POOL_W:]
    u_ref[...] = u
    p_ref[...] = pf

    outs = []
    for gi, win in enumerate(POOL_WINDOWS):
        cols = slice(gi * POOL_GC, (gi + 1) * POOL_GC)
        acc = u[:, cols]
        for j in range(1, win):
            acc = acc + sp_ref[POOL_BUF - j, :, cols]
        d = acc * (1.0 / win) - u[:, cols]
        outs.append(_dot(d.astype(BF16), pw_ref[gi]))
    po_ref[...] = (jnp.concatenate(outs, axis=1) * ps_ref[...]).astype(po_ref.dtype)

    bones = bones_ref[...]
    r, k2, v, logw, a, g, kk = _rwkv_prelude(
        pf, sh_ref[...], mu_ref[...], w0_ref[...], a0_ref[...], kk_ref[...], ka_ref[...],
        w2_ref[...], glup_ref[...], bones)
    w = jnp.exp(logw)
    aa = -kk
    bb = kk * a
    q = w * r + aa * _head_sum(bb * r, bones)
    yv = v * _head_sum(k2 * r, bones)
    vec_ref[0] = aa
    vec_ref[1] = q
    vec_ref[2] = w
    vec_ref[3] = bb
    vec_ref[4] = k2
    vec_ref[5] = v
    vec_ref[6] = yv
    vec_ref[7] = r
    gate_ref[...] = g


def _sample_front(x, sp, sh, g1, win, pw, ps, rows_prm, w2, gl_up, bones):
    n = x.shape[0]
    mu, w0, a0, k_k, k_a = rows_prm
    return pl.pallas_call(
        _sample_front_kernel,
        out_shape=[
            jax.ShapeDtypeStruct((n, POOL_W), F32),
            jax.ShapeDtypeStruct((n, SHIFT_W), F32),
            jax.ShapeDtypeStruct((n, POOL_W), BF16),
            jax.ShapeDtypeStruct((8, n, RWKV_W), F32),
            jax.ShapeDtypeStruct((n, RWKV_W), F32),
        ],
        compiler_params=pltpu.CompilerParams(vmem_limit_bytes=VMEM_LIMIT),
        name="sample_front",
    )(x, sp, sh, g1, win, pw, ps, mu, w0, a0, k_k, k_a, w2, gl_up, bones)


def _decode_state_kernel(s_ref, vec_ref, vt_ref, snew_ref, yt_ref):
    nb = s_ref.shape[0]
    for b in range(nb):
        for h in range(N_HEADS):
            j = b * N_HEADS + h
            s0 = s_ref[b, h]
            aa = vec_ref[0, b, h:h + 1, :]
            q = vec_ref[1, b, h:h + 1, :]
            w = vec_ref[2, b, h:h + 1, :]
            bb = vec_ref[3, b, h:h + 1, :]
            k = vec_ref[4, b, h:h + 1, :]
            vcol = vt_ref[:, j:j + 1]
            sa = jnp.sum(s0 * aa, axis=-1, keepdims=True)
            yt_ref[:, j:j + 1] = jnp.sum(s0 * q, axis=-1, keepdims=True)
            snew_ref[b, h] = s0 * w + sa * bb + vcol * k


def _decode_state(s, vec, vt):
    n = s.shape[0]
    db = DEC_BLOCK
    return pl.pallas_call(
        _decode_state_kernel,
        grid=(n // db,),
        in_specs=[
            pl.BlockSpec((db, N_HEADS, HEAD_DIM, HEAD_DIM), lambda i: (i, 0, 0, 0)),
            pl.BlockSpec((8, db, N_HEADS, HEAD_DIM), lambda i: (0, i, 0, 0)),
            pl.BlockSpec((None, HEAD_DIM, db * N_HEADS), lambda i: (i, 0, 0)),
        ],
        out_specs=[
            pl.BlockSpec((db, N_HEADS, HEAD_DIM, HEAD_DIM), lambda i: (i, 0, 0, 0)),
            pl.BlockSpec((None, HEAD_DIM, db * N_HEADS), lambda i: (i, 0, 0)),
        ],
        out_shape=[
            jax.ShapeDtypeStruct(s.shape, F32),
            jax.ShapeDtypeStruct((n // db, HEAD_DIM, db * N_HEADS), F32),
        ],
        compiler_params=pltpu.CompilerParams(
            dimension_semantics=("arbitrary",), vmem_limit_bytes=VMEM_LIMIT),
        name="decode_state",
    )(s, vec, vt)


def _sample_post_kernel(y0_ref, vec_ref, gate_ref, rk_ref, lng_ref, lnb_ref, bones_ref, rw_ref):
    y = y0_ref[...] + vec_ref[6]
    out = _rwkv_postlude(y, vec_ref[7], vec_ref[4], vec_ref[5], gate_ref[...],
                         rk_ref[...], lng_ref[...], lnb_ref[...], bones_ref[...])
    rw_ref[...] = out.astype(rw_ref.dtype)


def _sample_post(y0, vec, gate, r_k, ln_g, ln_b, bones):
    return pl.pallas_call(
        _sample_post_kernel,
        out_shape=jax.ShapeDtypeStruct(y0.shape, BF16),
        compiler_params=pltpu.CompilerParams(vmem_limit_bytes=VMEM_LIMIT),
        name="sample_post",
    )(y0, vec, gate, r_k, ln_g, ln_b, bones)


def kernel(x_prompt, x_sample, state_wkv, state_shift, state_pool, norm1_g, w_in, shift_mu, pool_w, pool_scale,
           w0, w_lora_up, a0, a_lora_up, g_lora_up, k_k, k_a, r_k, ln_x_g, ln_x_b, w_out, norm2_g, w_up, w_down,
           norm_f_g):
    depth = w_in.shape[0]
    assert depth == 1, "single-layer step"
    b, t, _ = x_prompt.shape
    n = x_sample.shape[0]
    row = lambda a: a.reshape(1, -1).astype(F32)

    g1 = row(norm1_g[0])
    win = w_in[0].astype(BF16)
    pw = pool_w[0].astype(BF16)
    ps = row(pool_scale[0])
    zeros = jnp.zeros((64, RWKV_W), F32)
    w2 = jnp.concatenate([jnp.concatenate([w_lora_up[0], zeros], axis=1),
                          jnp.concatenate([zeros, a_lora_up[0]], axis=1)], axis=0).astype(BF16)
    gl_up = g_lora_up[0].astype(BF16)
    bones = _head_ones()
    pre_rows = (row(shift_mu[0]), row(w0[0]), row(a0[0]), row(k_k[0]), row(k_a[0]))
    post_rows = (row(r_k[0]), row(ln_x_g[0]), row(ln_x_b[0]))
    wout = w_out[0].astype(BF16)
    g2 = row(norm2_g[0])
    wup = w_up[0].astype(BF16)
    wdn = w_down[0].astype(BF16)
    gf = row(norm_f_g)

    u_p, p_p, po_p = _inproj_pool(x_prompt, g1, win, pw, ps)
    rw_p, s_nat = _rwkv_chunked(p_p, pre_rows + post_rows, w2, gl_up, bones)
    y_p = _outproj_mlp(x_prompt.reshape(b * t, D_MODEL), po_p.reshape(b * t, POOL_W),
                       rw_p.reshape(b * t, RWKV_W), wout, g2, wup, wdn, gf).reshape(b, t, D_MODEL)
    wkv_p = jnp.transpose(s_nat.reshape(b, HEAD_DIM, N_HEADS, HEAD_DIM), (0, 2, 1, 3))[None]
    shift_p = p_p[:, t - 1:, :][None]
    pool_p = u_p[:, t - POOL_BUF:, :][None]

    xs = x_sample.reshape(n, D_MODEL)
    sp = state_pool[0]
    u_s, p_s, po_s, vec, gate = _sample_front(xs, jnp.transpose(sp, (1, 0, 2)), state_shift[0].reshape(n, SHIFT_W), g1, win, pw, ps,
                                              pre_rows, w2, gl_up, bones)
    vec4 = vec.reshape(8, n, N_HEADS, HEAD_DIM)
    nblk = n // DEC_BLOCK
    vt = jnp.transpose(vec4[5].reshape(nblk, DEC_BLOCK * N_HEADS, HEAD_DIM), (0, 2, 1))
    s_new, yt = _decode_state(state_wkv[0], vec4, vt)
    y0 = jnp.transpose(yt, (0, 2, 1)).reshape(n, RWKV_W)
    rw_s = _sample_post(y0, vec, gate, *post_rows, bones)
    y_s = _outproj_mlp(xs, po_s, rw_s, wout, g2, wup, wdn, gf).reshape(n, 1, D_MODEL)
    shift_s = p_s.reshape(1, n, 1, SHIFT_W)
    pool_s = jnp.concatenate([sp[:, 1:, :], u_s[:, None, :]], axis=1)[None]

    return (y_p, y_s, wkv_p, shift_p, pool_p, s_new[None], shift_s, pool_s)
```

```python
import functools

import jax
import jax.numpy as jnp
import numpy as np
from jax import lax
from jax.experimental import pallas as pl
from jax.experimental.pallas import tpu as pltpu

F32 = jnp.float32
BF16 = jnp.bfloat16

D_MODEL = 1024
POOL_W = 512
RWKV_W = 512
POOL_WINDOWS = (2, 4, 8, 16)
POOL_GC = 128
POOL_BUF = 15
HEAD_DIM = 64
N_HEADS = 8
SHIFT_W = 1792
IN_W = POOL_W + SHIFT_W
D_FF = 4096
LORA_Z0 = 3 * RWKV_W
GATE_Z0 = LORA_Z0 + 128
RMS_EPS = 1e-6
GN_EPS = 64e-5
NORM_EPS = 1e-12

CHUNK = 64
SEQ_BLOCK = 4
GROUP_HEADS = 4
GROUP_W = GROUP_HEADS * HEAD_DIM
N_GROUPS = N_HEADS // GROUP_HEADS
POOL_HALO = 16
ROW_BLOCK = 512
FF_BLOCK = 1024
DEC_BLOCK = 8
VMEM_LIMIT = 56 * 1024 * 1024


def _dot(a, b):
    return jnp.dot(a, b, preferred_element_type=F32)


def _dot_nt(a, b):
    return lax.dot_general(a, b, (((1,), (1,)), ((), ())), preferred_element_type=F32)


def _rmsnorm(x, g):
    return x * lax.rsqrt(jnp.mean(x * x, axis=-1, keepdims=True) + RMS_EPS) * g


def _sigmoid(x):
    return 1.0 / (1.0 + jnp.exp(-x))


def _softplus(x):
    return jnp.maximum(x, 0.0) + jnp.log(1.0 + jnp.exp(-jnp.abs(x)))


def _head_sum(x, bones):
    parts = [_dot(x[:, g * GROUP_W:(g + 1) * GROUP_W].astype(BF16), bones) for g in range(N_GROUPS)]
    return jnp.concatenate(parts, axis=1)


def _rwkv_prelude(pf, prev, mu, w0, a0, k_k, k_a, w2, gl_up, bones):
    ps = pf + (prev - pf) * mu
    r = ps[:, 0:RWKV_W]
    k = ps[:, RWKV_W:2 * RWKV_W]
    v = ps[:, 2 * RWKV_W:3 * RWKV_W]
    z = ps[:, LORA_Z0:GATE_Z0]
    lane = lax.broadcasted_iota(jnp.int32, z.shape, 1)
    zt = jnp.where(lane < 64, jnp.tanh(z), z)
    wa = _dot(zt.astype(BF16), w2)
    w_log = -_softplus(-(w0 + wa[:, :RWKV_W])) - 0.5
    logw = -jnp.exp(w_log)
    a = _sigmoid(a0 + wa[:, RWKV_W:])
    g = _dot(_sigmoid(ps[:, GATE_Z0:SHIFT_W]).astype(BF16), gl_up)
    kk = k * k_k
    kk = kk * lax.rsqrt(_head_sum(kk * kk, bones) + NORM_EPS)
    k2 = k * (1.0 + (a - 1.0) * k_a)
    return r, k2, v, logw, a, g, kk


def _rwkv_postlude(y, r, k2, v, g, r_k, ln_g, ln_b, bones):
    mean = _head_sum(y, bones) * (1.0 / HEAD_DIM)
    yc = y - mean
    var = _head_sum(yc * yc, bones) * (1.0 / HEAD_DIM)
    yn = yc * lax.rsqrt(var + GN_EPS) * ln_g + ln_b
    bonus = _head_sum(r * k2 * r_k, bones) * v
    return (yn + bonus) * g


def _inproj_pool_kernel(x_ref, g1_ref, win_ref, pw_ref, ps_ref, u_ref, p_ref, po_ref, ext_ref):
    tb = pl.program_id(1)
    rows = x_ref.shape[0]

    @pl.when(tb == 0)
    def _():
        ext_ref[0:POOL_HALO, :] = jnp.zeros((POOL_HALO, POOL_W), F32)

    h = _rmsnorm(x_ref[...], g1_ref[...]).astype(BF16)
    proj = _dot(h, win_ref[...])
    u = proj[:, :POOL_W]
    u_ref[...] = u
    p_ref[...] = proj[:, POOL_W:]
    ext_ref[POOL_HALO:POOL_HALO + rows, :] = u

    pos = tb * rows + lax.broadcasted_iota(jnp.int32, (rows, 1), 0)
    outs = []
    for gi, win in enumerate(POOL_WINDOWS):
        cols = slice(gi * POOL_GC, (gi + 1) * POOL_GC)
        acc = u[:, cols]
        for j in range(1, win):
            acc = acc + ext_ref[POOL_HALO - j:POOL_HALO - j + rows, cols]
        cnt = jnp.minimum(win, pos + 1).astype(F32)
        d = acc / cnt - u[:, cols]
        outs.append(_dot(d.astype(BF16), pw_ref[gi]))
    po_ref[...] = (jnp.concatenate(outs, axis=1) * ps_ref[...]).astype(po_ref.dtype)
    ext_ref[0:POOL_HALO, :] = ext_ref[rows:rows + POOL_HALO, :]


def _inproj_pool(x, g1, win, pw, ps):
    b, t, _ = x.shape
    rows = ROW_BLOCK
    const = lambda *shape: pl.BlockSpec(shape, lambda i, j: (0,) * len(shape))
    return pl.pallas_call(
        _inproj_pool_kernel,
        grid=(b, t // rows),
        in_specs=[
            pl.BlockSpec((None, rows, D_MODEL), lambda i, j: (i, j, 0)),
            const(1, D_MODEL),
            const(D_MODEL, IN_W),
            const(len(POOL_WINDOWS), POOL_GC, POOL_GC),
            const(1, POOL_W),
        ],
        out_specs=[
            pl.BlockSpec((None, rows, POOL_W), lambda i, j: (i, j, 0)),
            pl.BlockSpec((None, rows, SHIFT_W), lambda i, j: (i, j, 0)),
            pl.BlockSpec((None, rows, POOL_W), lambda i, j: (i, j, 0)),
        ],
        out_shape=[
            jax.ShapeDtypeStruct((b, t, POOL_W), F32),
            jax.ShapeDtypeStruct((b, t, SHIFT_W), F32),
            jax.ShapeDtypeStruct((b, t, POOL_W), BF16),
        ],
        scratch_shapes=[pltpu.VMEM((rows + POOL_HALO, POOL_W), F32)],
        compiler_params=pltpu.CompilerParams(
            dimension_semantics=("arbitrary", "arbitrary"), vmem_limit_bytes=VMEM_LIMIT),
        name="inproj_pool",
    )(x, g1, win, pw, ps)


def _expand(x_bf, hm):
    return jnp.concatenate([x_bf] * GROUP_HEADS, axis=0) * hm


def _chunk_groups(chains, strict, incl, iden, hm):
    c = chains[0][1].shape[0]
    each = lambda fn, *lists: [fn(*args) for args in zip(*lists)]
    blockdiag = lambda x: _expand(x.astype(BF16), hm)
    bf = lambda x: x.astype(BF16)
    s0, r, k2, v, aa, bb, w_in, w_ex, w_inv = (list(x) for x in zip(*chains))

    at = each(lambda a_, w_: (a_ * w_).astype(BF16), aa, w_ex)
    rt = each(lambda r_, w_: r_ * w_, r, w_in)
    bt = each(lambda b_, w_: (b_ * w_).astype(BF16), bb, w_inv)
    kt = each(lambda k_, w_: (k_ * w_).astype(BF16), k2, w_inv)
    e_b = each(lambda x: _expand(x, hm), bt)
    e_k = each(lambda x: _expand(x, hm), kt)
    e_v = each(lambda x: _expand(x.astype(BF16), hm), v)
    e_a = each(lambda x: _expand(x, hm), at)
    e_bk = each(lambda x, y: jnp.concatenate([x, y], axis=0), e_b, e_k)
    sc = each(lambda a_, r_, e_: _dot_nt(jnp.concatenate([a_, bf(r_)], axis=0), e_), at, rt, e_bk)
    l_ab = each(lambda x: x[:c, :GROUP_W] * strict, sc)
    l_ak = each(lambda x: x[:c, GROUP_W:] * strict, sc)
    m_rb = each(lambda x: x[c:, :GROUP_W] * incl, sc)
    m_rk = each(lambda x: x[c:, GROUP_W:] * incl, sc)

    n = each(lambda x: iden + x, l_ab)
    p = each(lambda x: _dot(bf(x), blockdiag(x)), l_ab)
    for _ in range(int(np.log2(c)) - 2):
        out = each(lambda p_, n_: _dot(bf(jnp.concatenate([p_, n_], axis=0)), blockdiag(p_)), p, n)
        p = each(lambda x: x[:c], out)
        n = each(lambda n_, x: n_ + x[c:], n, out)
    t_inv = each(lambda n_, p_: n_ + _dot(bf(n_), blockdiag(p_)), n, p)

    l_akv = each(lambda l_, e_: _dot(bf(l_), e_), l_ak, e_v)
    tau = each(lambda t_, e_, l_: _dot(bf(t_), jnp.concatenate([e_, blockdiag(l_)], axis=1)), t_inv, e_a, l_akv)
    e_tu = each(lambda x: jnp.concatenate([blockdiag(x[:, :GROUP_W]), blockdiag(x[:, GROUP_W:])], axis=1), tau)
    m1 = each(lambda m_, e_: _dot(bf(m_), e_), m_rb, e_tu)
    q = each(lambda r_, m_: r_ + m_[:, :GROUP_W], rt, m1)
    y0 = each(lambda m_, k_, e_: m_[:, GROUP_W:] + _dot(bf(k_), e_), m1, m_rk, e_v)
    iden_bf = bf(iden)
    tr = each(lambda e_, ev_: _dot_nt(iden_bf, jnp.concatenate([e_[:, :GROUP_W], e_[:, GROUP_W:], ev_], axis=0)),
              e_tu, e_v)
    g_t = each(lambda t_, e_: _dot(bf(t_[:, :GROUP_W]), e_), tr, e_b)
    h_t = each(lambda t_, e_: _dot(bf(t_[:, GROUP_W:]), e_), tr, e_bk)
    s_bf = each(bf, s0)
    y = each(lambda q_, s_, y_: _dot_nt(bf(q_), _expand(s_, hm)) + y_, q, s_bf, y0)
    s_new = each(lambda s_, sb_, g_, h_, w_: (s_ + _dot(sb_, blockdiag(g_)) + h_) * w_[c - 1:c, :],
                 s0, s_bf, g_t, h_t, w_in)
    return y, s_new


def _rwkv_chunk_kernel(p_ref, mu_ref, w0_ref, a0_ref, kk_ref, ka_ref, rk_ref, lng_ref, lnb_ref,
                       w2_ref, glup_ref, bones_ref, strict_ref, incl_ref, iden_ref, hm_ref, tri_ref,
                       rw_ref, sfin_ref, s_scr, prev_scr):
    t = pl.program_id(1)
    nb, c, _ = p_ref.shape

    @pl.when(t == 0)
    def _():
        s_scr[...] = jnp.zeros_like(s_scr)
        prev_scr[...] = jnp.zeros_like(prev_scr)

    row = lax.broadcasted_iota(jnp.int32, (c, 1), 0)
    pfs = [p_ref[n] for n in range(nb)]
    prevs = [jnp.where(row == 0, prev_scr[n, 0:1, :], pltpu.roll(pfs[n], 1, 0)) for n in range(nb)]
    bones = bones_ref[...]
    r, k2, v, logw, a, g, kk = _rwkv_prelude(
        jnp.concatenate(pfs, axis=0), jnp.concatenate(prevs, axis=0),
        mu_ref[...], w0_ref[...], a0_ref[...], kk_ref[...], ka_ref[...],
        w2_ref[...], glup_ref[...], bones)
    aa = -kk
    bb = kk * a

    l1 = logw.astype(BF16)
    r1 = logw - l1.astype(F32)
    l2 = r1.astype(BF16)
    l3 = (r1 - l2.astype(F32)).astype(BF16)

    strict = strict_ref[...]
    incl = incl_ref[...]
    iden = iden_ref[...]
    hm = hm_ref[...]
    tri = tri_ref[...]
    chains = []
    for n in range(nb):
        rows = slice(n * c, (n + 1) * c)
        cl = _dot(tri, jnp.concatenate([l1[rows], l2[rows], l3[rows], jnp.zeros_like(l1[rows])], axis=0))
        w_in = jnp.exp(cl)
        w_ex = jnp.exp(cl - logw[rows])
        w_inv = jnp.exp(-cl)
        for gi in range(N_GROUPS):
            sl = slice(gi * GROUP_W, (gi + 1) * GROUP_W)
            chains.append((s_scr[n, :, sl], r[rows, sl], k2[rows, sl], v[rows, sl], aa[rows, sl], bb[rows, sl],
                           w_in[:, sl], w_ex[:, sl], w_inv[:, sl]))
        prev_scr[n, 0:1, :] = pfs[n][c - 1:c, :]
    y_c, s_c = _chunk_groups(chains, strict, incl, iden, hm)
    for n in range(nb):
        for gi in range(N_GROUPS):
            s_scr[n, :, gi * GROUP_W:(gi + 1) * GROUP_W] = s_c[n * N_GROUPS + gi]
    y = jnp.concatenate([jnp.concatenate(y_c[n * N_GROUPS:(n + 1) * N_GROUPS], axis=1) for n in range(nb)], axis=0)
    out = _rwkv_postlude(y, r, k2, v, g, rk_ref[...], lng_ref[...], lnb_ref[...], bones)
    rw_ref[...] = out.reshape(nb, c, RWKV_W).astype(rw_ref.dtype)

    @pl.when(t == pl.num_programs(1) - 1)
    def _():
        sfin_ref[...] = s_scr[...]


def _chunk_constants():
    c = CHUNK
    t = np.arange(c)[:, None]
    s = (np.arange(GROUP_HEADS * c) % c)[None, :]
    strict = (s < t).astype(np.float32)
    incl = (s <= t).astype(np.float32)
    iden = (s == t).astype(np.float32)
    row_h = np.arange(GROUP_HEADS * c)[:, None] // c
    col_h = np.arange(GROUP_W)[None, :] // HEAD_DIM
    hm = (row_h == col_h).astype(np.float32)
    tri = (np.arange(c)[None, :] <= np.arange(c)[:, None]).astype(np.float32)
    tri = np.concatenate([tri, tri, tri, np.zeros_like(tri)], axis=1)
    return (jnp.asarray(strict), jnp.asarray(incl), jnp.asarray(iden),
            jnp.asarray(hm, dtype=BF16), jnp.asarray(tri, dtype=BF16))


def _head_ones():
    h = np.arange(GROUP_W) // HEAD_DIM
    return jnp.asarray((h[:, None] == h[None, :]).astype(np.float32), dtype=BF16)


def _rwkv_chunked(p, rows_prm, w2, gl_up, bones):
    b, t, _ = p.shape
    c = CHUNK
    nb = SEQ_BLOCK
    consts = _chunk_constants()
    const2 = lambda arr: pl.BlockSpec(arr.shape, lambda i, j: (0, 0))
    operands = list(rows_prm) + [w2, gl_up, bones] + list(consts)
    return pl.pallas_call(
        _rwkv_chunk_kernel,
        grid=(b // nb, t // c),
        in_specs=[pl.BlockSpec((nb, c, SHIFT_W), lambda i, j: (i, j, 0))] + [const2(a) for a in operands],
        out_specs=[
            pl.BlockSpec((nb, c, RWKV_W), lambda i, j: (i, j, 0)),
            pl.BlockSpec((nb, HEAD_DIM, RWKV_W), lambda i, j: (i, 0, 0)),
        ],
        out_shape=[
            jax.ShapeDtypeStruct((b, t, RWKV_W), BF16),
            jax.ShapeDtypeStruct((b, HEAD_DIM, RWKV_W), F32),
        ],
        scratch_shapes=[pltpu.VMEM((nb, HEAD_DIM, RWKV_W), F32), pltpu.VMEM((nb, 8, SHIFT_W), F32)],
        compiler_params=pltpu.CompilerParams(
            dimension_semantics=("arbitrary", "arbitrary"), vmem_limit_bytes=VMEM_LIMIT),
        name="rwkv_chunked",
    )(p, *operands)


def _outproj_mlp_kernel(x_ref, po_ref, rw_ref, wout_ref, g2_ref, wup_ref, wdn_ref, gf_ref, y_ref):
    mix = _dot(po_ref[...], wout_ref[0:POOL_W, :]) + _dot(rw_ref[...], wout_ref[POOL_W:, :])
    x1 = x_ref[...] + mix
    h2 = _rmsnorm(x1, g2_ref[...]).astype(BF16)
    acc = jnp.zeros_like(x1)
    for j in range(D_FF // FF_BLOCK):
        cols = slice(j * FF_BLOCK, (j + 1) * FF_BLOCK)
        hid = jnp.maximum(_dot(h2, wup_ref[:, cols]), 0.0)
        acc = acc + _dot((hid * hid).astype(BF16), wdn_ref[cols, :])
    y_ref[...] = _rmsnorm(x1 + acc, gf_ref[...])


def _outproj_mlp(x, po, rw, wout, g2, wup, wdn, gf):
    n = x.shape[0]
    rows = min(ROW_BLOCK, n)
    const = lambda *shape: pl.BlockSpec(shape, lambda i: (0,) * len(shape))
    return pl.pallas_call(
        _outproj_mlp_kernel,
        grid=(n // rows,),
        in_specs=[
            pl.BlockSpec((rows, D_MODEL), lambda i: (i, 0)),
            pl.BlockSpec((rows, POOL_W), lambda i: (i, 0)),
            pl.BlockSpec((rows, RWKV_W), lambda i: (i, 0)),
            const(D_MODEL, D_MODEL),
            const(1, D_MODEL),
            const(D_MODEL, D_FF),
            const(D_FF, D_MODEL),
            const(1, D_MODEL),
        ],
        out_specs=pl.BlockSpec((rows, D_MODEL), lambda i: (i, 0)),
        out_shape=jax.ShapeDtypeStruct((n, D_MODEL), F32),
        compiler_params=pltpu.CompilerParams(
            dimension_semantics=("arbitrary",), vmem_limit_bytes=VMEM_LIMIT),
        name="outproj_mlp",
    )(x, po, rw, wout, g2, wup, wdn, gf)


def _sample_front_kernel(x_ref, sp_ref, sh_ref, g1_ref, win_ref, pw_ref, ps_ref, mu_ref, w0_ref, a0_ref,
                         kk_ref, ka_ref, w2_ref, glup_ref, bones_ref,
                         u_ref, p_ref, po_ref, vec_ref, gate_ref):
    h = _rmsnorm(x_ref[...], g1_ref[...]).astype(BF16)
    proj = _dot(h, win_ref[...])
    u = proj[:, :POOL_W]
    pf = proj[:, POOL_W:]
    u_ref[...] = u
    p_ref[...] = pf

    outs = []
    for gi, win in enumerate(POOL_WINDOWS):
        cols = slice(gi * POOL_GC, (gi + 1) * POOL_GC)
        acc = u[:, cols]
        for j in range(1, win):
            acc = acc + sp_ref[POOL_BUF - j, :, cols]
        d = acc * (1.0 / win) - u[:, cols]
        outs.append(_dot(d.astype(BF16), pw_ref[gi]))
    po_ref[...] = (jnp.concatenate(outs, axis=1) * ps_ref[...]).astype(po_ref.dtype)

    bones = bones_ref[...]
    r, k2, v, logw, a, g, kk = _rwkv_prelude(
        pf, sh_ref[...], mu_ref[...], w0_ref[...], a0_ref[...], kk_ref[...], ka_ref[...],
        w2_ref[...], glup_ref[...], bones)
    w = jnp.exp(logw)
    aa = -kk
    bb = kk * a
    q = w * r + aa * _head_sum(bb * r, bones)
    yv = v * _head_sum(k2 * r, bones)
    vec_ref[0] = aa
    vec_ref[1] = q
    vec_ref[2] = w
    vec_ref[3] = bb
    vec_ref[4] = k2
    vec_ref[5] = v
    vec_ref[6] = yv
    vec_ref[7] = r
    gate_ref[...] = g


def _sample_front(x, sp, sh, g1, win, pw, ps, rows_prm, w2, gl_up, bones):
    n = x.shape[0]
    mu, w0, a0, k_k, k_a = rows_prm
    return pl.pallas_call(
        _sample_front_kernel,
        out_shape=[
            jax.ShapeDtypeStruct((n, POOL_W), F32),
            jax.ShapeDtypeStruct((n, SHIFT_W), F32),
            jax.ShapeDtypeStruct((n, POOL_W), BF16),
            jax.ShapeDtypeStruct((8, n, RWKV_W), F32),
            jax.ShapeDtypeStruct((n, RWKV_W), F32),
        ],
        compiler_params=pltpu.CompilerParams(vmem_limit_bytes=VMEM_LIMIT),
        name="sample_front",
    )(x, sp, sh, g1, win, pw, ps, mu, w0, a0, k_k, k_a, w2, gl_up, bones)


def _decode_state_kernel(s_ref, vec_ref, vt_ref, snew_ref, yt_ref):
    nb = s_ref.shape[0]
    for b in range(nb):
        for h in range(N_HEADS):
            j = b * N_HEADS + h
            s0 = s_ref[b, h]
            aa = vec_ref[0, b, h:h + 1, :]
            q = vec_ref[1, b, h:h + 1, :]
            w = vec_ref[2, b, h:h + 1, :]
            bb = vec_ref[3, b, h:h + 1, :]
            k = vec_ref[4, b, h:h + 1, :]
            vcol = vt_ref[:, j:j + 1]
            sa = jnp.sum(s0 * aa, axis=-1, keepdims=True)
            yt_ref[:, j:j + 1] = jnp.sum(s0 * q, axis=-1, keepdims=True)
            snew_ref[b, h] = s0 * w + sa * bb + vcol * k


def _decode_state(s, vec, vt):
    n = s.shape[0]
    db = DEC_BLOCK
    return pl.pallas_call(
        _decode_state_kernel,
        grid=(n // db,),
        in_specs=[
            pl.BlockSpec((db, N_HEADS, HEAD_DIM, HEAD_DIM), lambda i: (i, 0, 0, 0)),
            pl.BlockSpec((8, db, N_HEADS, HEAD_DIM), lambda i: (0, i, 0, 0)),
            pl.BlockSpec((None, HEAD_DIM, db * N_HEADS), lambda i: (i, 0, 0)),
        ],
        out_specs=[
            pl.BlockSpec((db, N_HEADS, HEAD_DIM, HEAD_DIM), lambda i: (i, 0, 0, 0)),
            pl.BlockSpec((None, HEAD_DIM, db * N_HEADS), lambda i: (i, 0, 0)),
        ],
        out_shape=[
            jax.ShapeDtypeStruct(s.shape, F32),
            jax.ShapeDtypeStruct((n // db, HEAD_DIM, db * N_HEADS), F32),
        ],
        compiler_params=pltpu.CompilerParams(
            dimension_semantics=("arbitrary",), vmem_limit_bytes=VMEM_LIMIT),
        name="decode_state",
    )(s, vec, vt)


def _sample_post_kernel(y0_ref, vec_ref, gate_ref, rk_ref, lng_ref, lnb_ref, bones_ref, rw_ref):
    y = y0_ref[...] + vec_ref[6]
    out = _rwkv_postlude(y, vec_ref[7], vec_ref[4], vec_ref[5], gate_ref[...],
                         rk_ref[...], lng_ref[...], lnb_ref[...], bones_ref[...])
    rw_ref[...] = out.astype(rw_ref.dtype)


def _sample_post(y0, vec, gate, r_k, ln_g, ln_b, bones):
    return pl.pallas_call(
        _sample_post_kernel,
        out_shape=jax.ShapeDtypeStruct(y0.shape, BF16),
        compiler_params=pltpu.CompilerParams(vmem_limit_bytes=VMEM_LIMIT),
        name="sample_post",
    )(y0, vec, gate, r_k, ln_g, ln_b, bones)


def kernel(x_prompt, x_sample, state_wkv, state_shift, state_pool, norm1_g, w_in, shift_mu, pool_w, pool_scale,
           w0, w_lora_up, a0, a_lora_up, g_lora_up, k_k, k_a, r_k, ln_x_g, ln_x_b, w_out, norm2_g, w_up, w_down,
           norm_f_g):
    depth = w_in.shape[0]
    assert depth == 1, "single-layer step"
    b, t, _ = x_prompt.shape
    n = x_sample.shape[0]
    row = lambda a: a.reshape(1, -1).astype(F32)

    g1 = row(norm1_g[0])
    win = w_in[0].astype(BF16)
    pw = pool_w[0].astype(BF16)
    ps = row(pool_scale[0])
    zeros = jnp.zeros((64, RWKV_W), F32)
    w2 = jnp.concatenate([jnp.concatenate([w_lora_up[0], zeros], axis=1),
                          jnp.concatenate([zeros, a_lora_up[0]], axis=1)], axis=0).astype(BF16)
    gl_up = g_lora_up[0].astype(BF16)
    bones = _head_ones()
    pre_rows = (row(shift_mu[0]), row(w0[0]), row(a0[0]), row(k_k[0]), row(k_a[0]))
    post_rows = (row(r_k[0]), row(ln_x_g[0]), row(ln_x_b[0]))
    wout = w_out[0].astype(BF16)
    g2 = row(norm2_g[0])
    wup = w_up[0].astype(BF16)
    wdn = w_down[0].astype(BF16)
    gf = row(norm_f_g)

    u_p, p_p, po_p = _inproj_pool(x_prompt, g1, win, pw, ps)
    rw_p, s_nat = _rwkv_chunked(p_p, pre_rows + post_rows, w2, gl_up, bones)
    y_p = _outproj_mlp(x_prompt.reshape(b * t, D_MODEL), po_p.reshape(b * t, POOL_W),
                       rw_p.reshape(b * t, RWKV_W), wout, g2, wup, wdn, gf).reshape(b, t, D_MODEL)
    wkv_p = jnp.transpose(s_nat.reshape(b, HEAD_DIM, N_HEADS, HEAD_DIM), (0, 2, 1, 3))[None]
    shift_p = p_p[:, t - 1:, :][None]
    pool_p = u_p[:, t - POOL_BUF:, :][None]

    xs = x_sample.reshape(n, D_MODEL)
    sp = state_pool[0]
    u_s, p_s, po_s, vec, gate = _sample_front(xs, jnp.transpose(sp, (1, 0, 2)), state_shift[0].reshape(n, SHIFT_W), g1, win, pw, ps,
                                              pre_rows, w2, gl_up, bones)
    vec4 = vec.reshape(8, n, N_HEADS, HEAD_DIM)
    nblk = n // DEC_BLOCK
    vt = jnp.transpose(vec4[5].reshape(nblk, DEC_BLOCK * N_HEADS, HEAD_DIM), (0, 2, 1))
    s_new, yt = _decode_state(state_wkv[0], vec4, vt)
    y0 = jnp.transpose(yt, (0, 2, 1)).reshape(n, RWKV_W)
    rw_s = _sample_post(y0, vec, gate, *post_rows, bones)
    y_s = _outproj_mlp(xs, po_s, rw_s, wout, g2, wup, wdn, gf).reshape(n, 1, D_MODEL)
    shift_s = p_s.reshape(1, n, 1, SHIFT_W)
    pool_s = jnp.concatenate([sp[:, 1:, :], u_s[:, None, :]], axis=1)[None]

    return (y_p, y_s, wkv_p, shift_p, pool_p, s_new[None], shift_s, pool_s)
```

```python
import functools

import jax
import jax.numpy as jnp
import numpy as np
from jax import lax
from jax.experimental import pallas as pl
from jax.experimental.pallas import tpu as pltpu

F32 = jnp.float32
BF16 = jnp.bfloat16

D_MODEL = 1024
POOL_W = 512
RWKV_W = 512
POOL_WINDOWS = (2, 4, 8, 16)
POOL_GC = 128
POOL_BUF = 15
HEAD_DIM = 64
N_HEADS = 8
SHIFT_W = 1792
IN_W = POOL_W + SHIFT_W
D_FF = 4096
LORA_Z0 = 3 * RWKV_W
GATE_Z0 = LORA_Z0 + 128
RMS_EPS = 1e-6
GN_EPS = 64e-5
NORM_EPS = 1e-12

CHUNK = 64
SEQ_BLOCK = 4
GROUP_HEADS = 4
GROUP_W = GROUP_HEADS * HEAD_DIM
N_GROUPS = N_HEADS // GROUP_HEADS
POOL_HALO = 16
ROW_BLOCK = 512
FF_BLOCK = 1024
DEC_BLOCK = 8
VMEM_LIMIT = 56 * 1024 * 1024


def _dot(a, b):
    return jnp.dot(a, b, preferred_element_type=F32)


def _dot_nt(a, b):
    return lax.dot_general(a, b, (((1,), (1,)), ((), ())), preferred_element_type=F32)


def _rmsnorm(x, g):
    return x * lax.rsqrt(jnp.mean(x * x, axis=-1, keepdims=True) + RMS_EPS) * g


def _sigmoid(x):
    return 1.0 / (1.0 + jnp.exp(-x))


def _softplus(x):
    return jnp.maximum(x, 0.0) + jnp.log(1.0 + jnp.exp(-jnp.abs(x)))


def _head_sum(x, bones):
    parts = [_dot(x[:, g * GROUP_W:(g + 1) * GROUP_W].astype(BF16), bones) for g in range(N_GROUPS)]
    return jnp.concatenate(parts, axis=1)


def _rwkv_prelude(pf, prev, mu, w0, a0, k_k, k_a, w2, gl_up, bones):
    ps = pf + (prev - pf) * mu
    r = ps[:, 0:RWKV_W]
    k = ps[:, RWKV_W:2 * RWKV_W]
    v = ps[:, 2 * RWKV_W:3 * RWKV_W]
    z = ps[:, LORA_Z0:GATE_Z0]
    lane = lax.broadcasted_iota(jnp.int32, z.shape, 1)
    zt = jnp.where(lane < 64, jnp.tanh(z), z)
    wa = _dot(zt.astype(BF16), w2)
    w_log = -_softplus(-(w0 + wa[:, :RWKV_W])) - 0.5
    logw = -jnp.exp(w_log)
    a = _sigmoid(a0 + wa[:, RWKV_W:])
    g = _dot(_sigmoid(ps[:, GATE_Z0:SHIFT_W]).astype(BF16), gl_up)
    kk = k * k_k
    kk = kk * lax.rsqrt(_head_sum(kk * kk, bones) + NORM_EPS)
    k2 = k * (1.0 + (a - 1.0) * k_a)
    return r, k2, v, logw, a, g, kk


def _rwkv_postlude(y, r, k2, v, g, r_k, ln_g, ln_b, bones):
    mean = _head_sum(y, bones) * (1.0 / HEAD_DIM)
    yc = y - mean
    var = _head_sum(yc * yc, bones) * (1.0 / HEAD_DIM)
    yn = yc * lax.rsqrt(var + GN_EPS) * ln_g + ln_b
    bonus = _head_sum(r * k2 * r_k, bones) * v
    return (yn + bonus) * g


def _inproj_pool_kernel(x_ref, g1_ref, win_ref, pw_ref, ps_ref, u_ref, p_ref, po_ref, ext_ref):
    tb = pl.program_id(1)
    rows = x_ref.shape[0]

    @pl.when(tb == 0)
    def _():
        ext_ref[0:POOL_HALO, :] = jnp.zeros((POOL_HALO, POOL_W), F32)

    h = _rmsnorm(x_ref[...], g1_ref[...]).astype(BF16)
    proj = _dot(h, win_ref[...])
    u = proj[:, :POOL_W]
    u_ref[...] = u
    p_ref[...] = proj[:, POOL_W:]
    ext_ref[POOL_HALO:POOL_HALO + rows, :] = u

    pos = tb * rows + lax.broadcasted_iota(jnp.int32, (rows, 1), 0)
    outs = []
    for gi, win in enumerate(POOL_WINDOWS):
        cols = slice(gi * POOL_GC, (gi + 1) * POOL_GC)
        acc = u[:, cols]
        for j in range(1, win):
            acc = acc + ext_ref[POOL_HALO - j:POOL_HALO - j + rows, cols]
        cnt = jnp.minimum(win, pos + 1).astype(F32)
        d = acc / cnt - u[:, cols]
        outs.append(_dot(d.astype(BF16), pw_ref[gi]))
    po_ref[...] = (jnp.concatenate(outs, axis=1) * ps_ref[...]).astype(po_ref.dtype)
    ext_ref[0:POOL_HALO, :] = ext_ref[rows:rows + POOL_HALO, :]


def _inproj_pool(x, g1, win, pw, ps):
    b, t, _ = x.shape
    rows = ROW_BLOCK
    const = lambda *shape: pl.BlockSpec(shape, lambda i, j: (0,) * len(shape))
    return pl.pallas_call(
        _inproj_pool_kernel,
        grid=(b, t // rows),
        in_specs=[
            pl.BlockSpec((None, rows, D_MODEL), lambda i, j: (i, j, 0)),
            const(1, D_MODEL),
            const(D_MODEL, IN_W),
            const(len(POOL_WINDOWS), POOL_GC, POOL_GC),
            const(1, POOL_W),
        ],
        out_specs=[
            pl.BlockSpec((None, rows, POOL_W), lambda i, j: (i, j, 0)),
            pl.BlockSpec((None, rows, SHIFT_W), lambda i, j: (i, j, 0)),
            pl.BlockSpec((None, rows, POOL_W), lambda i, j: (i, j, 0)),
        ],
        out_shape=[
            jax.ShapeDtypeStruct((b, t, POOL_W), F32),
            jax.ShapeDtypeStruct((b, t, SHIFT_W), F32),
            jax.ShapeDtypeStruct((b, t, POOL_W), BF16),
        ],
        scratch_shapes=[pltpu.VMEM((rows + POOL_HALO, POOL_W), F32)],
        compiler_params=pltpu.CompilerParams(
            dimension_semantics=("arbitrary", "arbitrary"), vmem_limit_bytes=VMEM_LIMIT),
        name="inproj_pool",
    )(x, g1, win, pw, ps)


def _expand(x_bf, half):
    zero = jnp.zeros((x_bf.shape[0], 128), x_bf.dtype)
    blocks = []
    for h in range(GROUP_HEADS):
        tile = h // 2
        piece = x_bf[:, tile * 128:(tile + 1) * 128] * half[h % 2]
        blocks.append(jnp.concatenate([piece, zero] if tile == 0 else [zero, piece], axis=1))
    return jnp.concatenate(blocks, axis=0)


def _diag_blocks(full, low):
    tiles = []
    for tile in range(GROUP_HEADS // 2):
        cols = slice(tile * 128, (tile + 1) * 128)
        left = full[(2 * tile) * HEAD_DIM:(2 * tile + 1) * HEAD_DIM, cols]
        right = full[(2 * tile + 1) * HEAD_DIM:(2 * tile + 2) * HEAD_DIM, cols]
        tiles.append(jnp.where(low, left, right))
    return jnp.concatenate(tiles, axis=1)


def _chunk_groups(chains, strict, incl, iden, half):
    c = chains[0][1].shape[0]
    each = lambda fn, *lists: [fn(*args) for args in zip(*lists)]
    bf = lambda x: x.astype(BF16)
    blockdiag = lambda x: _expand(bf(x), half)
    rows2 = lambda x, y: jnp.concatenate([x, y], axis=0)
    low = lax.broadcasted_iota(jnp.int32, (HEAD_DIM, 128), 1) < HEAD_DIM
    s0, r, k2, v, aa, bb, w_in, w_ex, w_inv = (list(x) for x in zip(*chains))

    at = each(lambda a_, w_: bf(a_ * w_), aa, w_ex)
    rt = each(lambda r_, w_: bf(r_ * w_), r, w_in)
    bt = each(lambda b_, w_: bf(b_ * w_), bb, w_inv)
    kt = each(lambda k_, w_: bf(k_ * w_), k2, w_inv)
    vb = each(bf, v)
    ar = each(rows2, at, rt)
    e_bk = each(lambda b_, k_: rows2(_expand(b_, half), _expand(k_, half)), bt, kt)
    sc = each(_dot_nt, ar, e_bk)
    l_ab = each(lambda x: x[:c, :GROUP_W] * strict, sc)
    lm_k = each(lambda x: bf(rows2(x[:c, GROUP_W:] * strict, x[c:, GROUP_W:] * incl)), sc)
    m_rb = each(lambda x: bf(x[c:, :GROUP_W] * incl), sc)

    n = each(lambda x: iden + x, l_ab)
    p = each(lambda x: _dot(bf(x), blockdiag(x)), l_ab)
    for _ in range(int(np.log2(c)) - 2):
        out = each(lambda p_, n_: _dot(bf(rows2(p_, n_)), blockdiag(p_)), p, n)
        p = each(lambda x: x[:c], out)
        n = each(lambda n_, x: n_ + x[c:], n, out)
    t_inv = each(lambda n_, p_: bf(n_ + _dot(bf(n_), blockdiag(p_))), n, p)

    lv = each(lambda m_, v_: _dot(m_, _expand(v_, half)), lm_k, vb)
    ps = each(lambda a_, s_: _dot_nt(a_, blockdiag(s_)), ar, s0)
    u = each(lambda t_, p_, l_: _dot(t_, blockdiag(p_[:c] + l_[:c])), t_inv, ps, lv)
    ub = each(bf, u)
    y = each(lambda p_, m_, u_, l_: p_[c:] + _dot(m_, _expand(u_, half)) + l_[c:], ps, m_rb, ub, lv)
    full = each(lambda u_, v_, b_, k_: lax.dot_general(rows2(u_, v_), rows2(b_, k_), (((0,), (0,)), ((), ())),
                                                       preferred_element_type=F32), ub, vb, bt, kt)
    s_new = each(lambda s_, f_, w_: (s_ + _diag_blocks(f_, low)) * w_[c - 1:c, :], s0, full, w_in)
    return y, s_new


def _rwkv_chunk_kernel(p_ref, mu_ref, w0_ref, a0_ref, kk_ref, ka_ref, rk_ref, lng_ref, lnb_ref,
                       w2_ref, glup_ref, bones_ref, strict_ref, incl_ref, iden_ref, half_ref, tri_ref,
                       rw_ref, sfin_ref, s_scr, prev_scr):
    t = pl.program_id(1)
    nb, c, _ = p_ref.shape

    @pl.when(t == 0)
    def _():
        s_scr[...] = jnp.zeros_like(s_scr)
        prev_scr[...] = jnp.zeros_like(prev_scr)

    row = lax.broadcasted_iota(jnp.int32, (c, 1), 0)
    pfs = [p_ref[n] for n in range(nb)]
    prevs = [jnp.where(row == 0, prev_scr[n, 0:1, :], pltpu.roll(pfs[n], 1, 0)) for n in range(nb)]
    bones = bones_ref[...]
    r, k2, v, logw, a, g, kk = _rwkv_prelude(
        jnp.concatenate(pfs, axis=0), jnp.concatenate(prevs, axis=0),
        mu_ref[...], w0_ref[...], a0_ref[...], kk_ref[...], ka_ref[...],
        w2_ref[...], glup_ref[...], bones)
    aa = -kk
    bb = kk * a

    l1 = logw.astype(BF16)
    r1 = logw - l1.astype(F32)
    l2 = r1.astype(BF16)
    l3 = (r1 - l2.astype(F32)).astype(BF16)

    strict = strict_ref[...]
    incl = incl_ref[...]
    iden = iden_ref[...]
    half = half_ref[...]
    tri = tri_ref[...]
    chains = []
    for n in range(nb):
        rows = slice(n * c, (n + 1) * c)
        cl = _dot(tri, jnp.concatenate([l1[rows], l2[rows], l3[rows], jnp.zeros_like(l1[rows])], axis=0))
        w_in = jnp.exp(cl)
        w_ex = jnp.exp(cl - logw[rows])
        w_inv = jnp.exp(-cl)
        for gi in range(N_GROUPS):
            sl = slice(gi * GROUP_W, (gi + 1) * GROUP_W)
            chains.append((s_scr[n, :, sl], r[rows, sl], k2[rows, sl], v[rows, sl], aa[rows, sl], bb[rows, sl],
                           w_in[:, sl], w_ex[:, sl], w_inv[:, sl]))
        prev_scr[n, 0:1, :] = pfs[n][c - 1:c, :]
    y_c, s_c = _chunk_groups(chains, strict, incl, iden, half)
    for n in range(nb):
        for gi in range(N_GROUPS):
            s_scr[n, :, gi * GROUP_W:(gi + 1) * GROUP_W] = s_c[n * N_GROUPS + gi]
    y = jnp.concatenate([jnp.concatenate(y_c[n * N_GROUPS:(n + 1) * N_GROUPS], axis=1) for n in range(nb)], axis=0)
    out = _rwkv_postlude(y, r, k2, v, g, rk_ref[...], lng_ref[...], lnb_ref[...], bones)
    rw_ref[...] = out.reshape(nb, c, RWKV_W).astype(rw_ref.dtype)

    @pl.when(t == pl.num_programs(1) - 1)
    def _():
        sfin_ref[...] = s_scr[...]


def _chunk_constants():
    c = CHUNK
    t = np.arange(c)[:, None]
    s = (np.arange(GROUP_HEADS * c) % c)[None, :]
    strict = (s < t).astype(np.float32)
    incl = (s <= t).astype(np.float32)
    iden = (s == t).astype(np.float32)
    lane_low = np.broadcast_to(np.arange(128)[None, :] < HEAD_DIM, (c, 128))
    half = np.stack([lane_low, ~lane_low]).astype(np.float32)
    tri = (np.arange(c)[None, :] <= np.arange(c)[:, None]).astype(np.float32)
    tri = np.concatenate([tri, tri, tri, np.zeros_like(tri)], axis=1)
    return (jnp.asarray(strict), jnp.asarray(incl), jnp.asarray(iden),
            jnp.asarray(half, dtype=BF16), jnp.asarray(tri, dtype=BF16))


def _head_ones():
    h = np.arange(GROUP_W) // HEAD_DIM
    return jnp.asarray((h[:, None] == h[None, :]).astype(np.float32), dtype=BF16)


def _rwkv_chunked(p, rows_prm, w2, gl_up, bones):
    b, t, _ = p.shape
    c = CHUNK
    nb = SEQ_BLOCK
    consts = _chunk_constants()
    const2 = lambda arr: pl.BlockSpec(arr.shape, lambda i, j: (0,) * arr.ndim)
    operands = list(rows_prm) + [w2, gl_up, bones] + list(consts)
    return pl.pallas_call(
        _rwkv_chunk_kernel,
        grid=(b // nb, t // c),
        in_specs=[pl.BlockSpec((nb, c, SHIFT_W), lambda i, j: (i, j, 0))] + [const2(a) for a in operands],
        out_specs=[
            pl.BlockSpec((nb, c, RWKV_W), lambda i, j: (i, j, 0)),
            pl.BlockSpec((nb, HEAD_DIM, RWKV_W), lambda i, j: (i, 0, 0)),
        ],
        out_shape=[
            jax.ShapeDtypeStruct((b, t, RWKV_W), BF16),
            jax.ShapeDtypeStruct((b, HEAD_DIM, RWKV_W), F32),
        ],
        scratch_shapes=[pltpu.VMEM((nb, HEAD_DIM, RWKV_W), F32), pltpu.VMEM((nb, 8, SHIFT_W), F32)],
        compiler_params=pltpu.CompilerParams(
            dimension_semantics=("arbitrary", "arbitrary"), vmem_limit_bytes=VMEM_LIMIT),
        name="rwkv_chunked",
    )(p, *operands)


def _outproj_mlp_kernel(x_ref, po_ref, rw_ref, wout_ref, g2_ref, wup_ref, wdn_ref, gf_ref, y_ref):
    mix = _dot(po_ref[...], wout_ref[0:POOL_W, :]) + _dot(rw_ref[...], wout_ref[POOL_W:, :])
    x1 = x_ref[...] + mix
    h2 = _rmsnorm(x1, g2_ref[...]).astype(BF16)
    acc = jnp.zeros_like(x1)
    for j in range(D_FF // FF_BLOCK):
        cols = slice(j * FF_BLOCK, (j + 1) * FF_BLOCK)
        hid = jnp.maximum(_dot(h2, wup_ref[:, cols]), 0.0)
        acc = acc + _dot((hid * hid).astype(BF16), wdn_ref[cols, :])
    y_ref[...] = _rmsnorm(x1 + acc, gf_ref[...])


def _outproj_mlp(x, po, rw, wout, g2, wup, wdn, gf):
    n = x.shape[0]
    rows = min(ROW_BLOCK, n)
    const = lambda *shape: pl.BlockSpec(shape, lambda i: (0,) * len(shape))
    return pl.pallas_call(
        _outproj_mlp_kernel,
        grid=(n // rows,),
        in_specs=[
            pl.BlockSpec((rows, D_MODEL), lambda i: (i, 0)),
            pl.BlockSpec((rows, POOL_W), lambda i: (i, 0)),
            pl.BlockSpec((rows, RWKV_W), lambda i: (i, 0)),
            const(D_MODEL, D_MODEL),
            const(1, D_MODEL),
            const(D_MODEL, D_FF),
            const(D_FF, D_MODEL),
            const(1, D_MODEL),
        ],
        out_specs=pl.BlockSpec((rows, D_MODEL), lambda i: (i, 0)),
        out_shape=jax.ShapeDtypeStruct((n, D_MODEL), F32),
        compiler_params=pltpu.CompilerParams(
            dimension_semantics=("arbitrary",), vmem_limit_bytes=VMEM_LIMIT),
        name="outproj_mlp",
    )(x, po, rw, wout, g2, wup, wdn, gf)


def _sample_front_kernel(x_ref, sp_ref, sh_ref, g1_ref, win_ref, pw_ref, ps_ref, mu_ref, w0_ref, a0_ref,
                         kk_ref, ka_ref, w2_ref, glup_ref, bones_ref,
                         u_ref, p_ref, po_ref, vec_ref, gate_ref):
    h = _rmsnorm(x_ref[...], g1_ref[...]).astype(BF16)
    proj = _dot(h, win_ref[...])
    u = proj[:, :POOL_W]
    pf = proj[:, POOL_W:]
    u_ref[...] = u
    p_ref[...] = pf

    outs = []
    for gi, win in enumerate(POOL_WINDOWS):
        cols = slice(gi * POOL_GC, (gi + 1) * POOL_GC)
        acc = u[:, cols]
        for j in range(1, win):
            acc = acc + sp_ref[POOL_BUF - j, :, cols]
        d = acc * (1.0 / win) - u[:, cols]
        outs.append(_dot(d.astype(BF16), pw_ref[gi]))
    po_ref[...] = (jnp.concatenate(outs, axis=1) * ps_ref[...]).astype(po_ref.dtype)

    bones = bones_ref[...]
    r, k2, v, logw, a, g, kk = _rwkv_prelude(
        pf, sh_ref[...], mu_ref[...], w0_ref[...], a0_ref[...], kk_ref[...], ka_ref[...],
        w2_ref[...], glup_ref[...], bones)
    w = jnp.exp(logw)
    aa = -kk
    bb = kk * a
    q = w * r + aa * _head_sum(bb * r, bones)
    yv = v * _head_sum(k2 * r, bones)
    vec_ref[0] = aa
    vec_ref[1] = q
    vec_ref[2] = w
    vec_ref[3] = bb
    vec_ref[4] = k2
    vec_ref[5] = v
    vec_ref[6] = yv
    vec_ref[7] = r
    gate_ref[...] = g


def _sample_front(x, sp, sh, g1, win, pw, ps, rows_prm, w2, gl_up, bones):
    n = x.shape[0]
    mu, w0, a0, k_k, k_a = rows_prm
    return pl.pallas_call(
        _sample_front_kernel,
        out_shape=[
            jax.ShapeDtypeStruct((n, POOL_W), F32),
            jax.ShapeDtypeStruct((n, SHIFT_W), F32),
            jax.ShapeDtypeStruct((n, POOL_W), BF16),
            jax.ShapeDtypeStruct((8, n, RWKV_W), F32),
            jax.ShapeDtypeStruct((n, RWKV_W), F32),
        ],
        compiler_params=pltpu.CompilerParams(vmem_limit_bytes=VMEM_LIMIT),
        name="sample_front",
    )(x, sp, sh, g1, win, pw, ps, mu, w0, a0, k_k, k_a, w2, gl_up, bones)


def _decode_state_kernel(s_ref, vec_ref, vt_ref, snew_ref, yt_ref):
    nb = s_ref.shape[0]
    for b in range(nb):
        for h in range(N_HEADS):
            j = b * N_HEADS + h
            s0 = s_ref[b, h]
            aa = vec_ref[0, b, h:h + 1, :]
            q = vec_ref[1, b, h:h + 1, :]
            w = vec_ref[2, b, h:h + 1, :]
            bb = vec_ref[3, b, h:h + 1, :]
            k = vec_ref[4, b, h:h + 1, :]
            vcol = vt_ref[:, j:j + 1]
            sa = jnp.sum(s0 * aa, axis=-1, keepdims=True)
            yt_ref[:, j:j + 1] = jnp.sum(s0 * q, axis=-1, keepdims=True)
            snew_ref[b, h] = s0 * w + sa * bb + vcol * k


def _decode_state(s, vec, vt):
    n = s.shape[0]
    db = DEC_BLOCK
    return pl.pallas_call(
        _decode_state_kernel,
        grid=(n // db,),
        in_specs=[
            pl.BlockSpec((db, N_HEADS, HEAD_DIM, HEAD_DIM), lambda i: (i, 0, 0, 0)),
            pl.BlockSpec((8, db, N_HEADS, HEAD_DIM), lambda i: (0, i, 0, 0)),
            pl.BlockSpec((None, HEAD_DIM, db * N_HEADS), lambda i: (i, 0, 0)),
        ],
        out_specs=[
            pl.BlockSpec((db, N_HEADS, HEAD_DIM, HEAD_DIM), lambda i: (i, 0, 0, 0)),
            pl.BlockSpec((None, HEAD_DIM, db * N_HEADS), lambda i: (i, 0, 0)),
        ],
        out_shape=[
            jax.ShapeDtypeStruct(s.shape, F32),
            jax.ShapeDtypeStruct((n // db, HEAD_DIM, db * N_HEADS), F32),
        ],
        compiler_params=pltpu.CompilerParams(
            dimension_semantics=("arbitrary",), vmem_limit_bytes=VMEM_LIMIT),
        name="decode_state",
    )(s, vec, vt)


def _sample_post_kernel(y0_ref, vec_ref, gate_ref, rk_ref, lng_ref, lnb_ref, bones_ref, rw_ref):
    y = y0_ref[...] + vec_ref[6]
    out = _rwkv_postlude(y, vec_ref[7], vec_ref[4], vec_ref[5], gate_ref[...],
                         rk_ref[...], lng_ref[...], lnb_ref[...], bones_ref[...])
    rw_ref[...] = out.astype(rw_ref.dtype)


def _sample_post(y0, vec, gate, r_k, ln_g, ln_b, bones):
    return pl.pallas_call(
        _sample_post_kernel,
        out_shape=jax.ShapeDtypeStruct(y0.shape, BF16),
        compiler_params=pltpu.CompilerParams(vmem_limit_bytes=VMEM_LIMIT),
        name="sample_post",
    )(y0, vec, gate, r_k, ln_g, ln_b, bones)


def kernel(x_prompt, x_sample, state_wkv, state_shift, state_pool, norm1_g, w_in, shift_mu, pool_w, pool_scale,
           w0, w_lora_up, a0, a_lora_up, g_lora_up, k_k, k_a, r_k, ln_x_g, ln_x_b, w_out, norm2_g, w_up, w_down,
           norm_f_g):
    depth = w_in.shape[0]
    assert depth == 1, "single-layer step"
    b, t, _ = x_prompt.shape
    n = x_sample.shape[0]
    row = lambda a: a.reshape(1, -1).astype(F32)

    g1 = row(norm1_g[0])
    win = w_in[0].astype(BF16)
    pw = pool_w[0].astype(BF16)
    ps = row(pool_scale[0])
    zeros = jnp.zeros((64, RWKV_W), F32)
    w2 = jnp.concatenate([jnp.concatenate([w_lora_up[0], zeros], axis=1),
                          jnp.concatenate([zeros, a_lora_up[0]], axis=1)], axis=0).astype(BF16)
    gl_up = g_lora_up[0].astype(BF16)
    bones = _head_ones()
    pre_rows = (row(shift_mu[0]), row(w0[0]), row(a0[0]), row(k_k[0]), row(k_a[0]))
    post_rows = (row(r_k[0]), row(ln_x_g[0]), row(ln_x_b[0]))
    wout = w_out[0].astype(BF16)
    g2 = row(norm2_g[0])
    wup = w_up[0].astype(BF16)
    wdn = w_down[0].astype(BF16)
    gf = row(norm_f_g)

    u_p, p_p, po_p = _inproj_pool(x_prompt, g1, win, pw, ps)
    rw_p, s_nat = _rwkv_chunked(p_p, pre_rows + post_rows, w2, gl_up, bones)
    y_p = _outproj_mlp(x_prompt.reshape(b * t, D_MODEL), po_p.reshape(b * t, POOL_W),
                       rw_p.reshape(b * t, RWKV_W), wout, g2, wup, wdn, gf).reshape(b, t, D_MODEL)
    wkv_p = jnp.transpose(s_nat.reshape(b, HEAD_DIM, N_HEADS, HEAD_DIM), (0, 2, 1, 3))[None]
    shift_p = p_p[:, t - 1:, :][None]
    pool_p = u_p[:, t - POOL_BUF:, :][None]

    xs = x_sample.reshape(n, D_MODEL)
    sp = state_pool[0]
    u_s, p_s, po_s, vec, gate = _sample_front(xs, jnp.transpose(sp, (1, 0, 2)), state_shift[0].reshape(n, SHIFT_W), g1, win, pw, ps,
                                              pre_rows, w2, gl_up, bones)
    vec4 = vec.reshape(8, n, N_HEADS, HEAD_DIM)
    nblk = n // DEC_BLOCK
    vt = jnp.transpose(vec4[5].reshape(nblk, DEC_BLOCK * N_HEADS, HEAD_DIM), (0, 2, 1))
    s_new, yt = _decode_state(state_wkv[0], vec4, vt)
    y0 = jnp.transpose(yt, (0, 2, 1)).reshape(n, RWKV_W)
    rw_s = _sample_post(y0, vec, gate, *post_rows, bones)
    y_s = _outproj_mlp(xs, po_s, rw_s, wout, g2, wup, wdn, gf).reshape(n, 1, D_MODEL)
    shift_s = p_s.reshape(1, n, 1, SHIFT_W)
    pool_s = jnp.concatenate([sp[:, 1:, :], u_s[:, None, :]], axis=1)[None]

    return (y_p, y_s, wkv_p, shift_p, pool_p, s_new[None], shift_s, pool_s)
```

```python
import functools

import jax
import jax.numpy as jnp
import numpy as np
from jax import lax
from jax.experimental import pallas as pl
from jax.experimental.pallas import tpu as pltpu

F32 = jnp.float32
BF16 = jnp.bfloat16

D_MODEL = 1024
POOL_W = 512
RWKV_W = 512
POOL_WINDOWS = (2, 4, 8, 16)
POOL_GC = 128
POOL_BUF = 15
HEAD_DIM = 64
N_HEADS = 8
SHIFT_W = 1792
IN_W = POOL_W + SHIFT_W
D_FF = 4096
LORA_Z0 = 3 * RWKV_W
GATE_Z0 = LORA_Z0 + 128
RMS_EPS = 1e-6
GN_EPS = 64e-5
NORM_EPS = 1e-12

CHUNK = 64
SEQ_BLOCK = 4
GROUP_HEADS = 4
GROUP_W = GROUP_HEADS * HEAD_DIM
N_GROUPS = N_HEADS // GROUP_HEADS
POOL_HALO = 16
ROW_BLOCK = 512
FF_BLOCK = 1024
DEC_ROWS = 32
VMEM_LIMIT = 56 * 1024 * 1024


def _dot(a, b):
    return jnp.dot(a, b, preferred_element_type=F32)


def _dot_nt(a, b):
    return lax.dot_general(a, b, (((1,), (1,)), ((), ())), preferred_element_type=F32)


def _rmsnorm(x, g):
    return x * lax.rsqrt(jnp.mean(x * x, axis=-1, keepdims=True) + RMS_EPS) * g


def _sigmoid(x):
    return 1.0 / (1.0 + jnp.exp(-x))


def _softplus(x):
    return jnp.maximum(x, 0.0) + jnp.log(1.0 + jnp.exp(-jnp.abs(x)))


def _head_sum(x, bones):
    parts = [_dot(x[:, g * GROUP_W:(g + 1) * GROUP_W].astype(BF16), bones) for g in range(N_GROUPS)]
    return jnp.concatenate(parts, axis=1)


def _rwkv_prelude(pf, prev, mu, w0, a0, k_k, k_a, w2, gl_up, bones):
    ps = pf + (prev - pf) * mu
    r = ps[:, 0:RWKV_W]
    k = ps[:, RWKV_W:2 * RWKV_W]
    v = ps[:, 2 * RWKV_W:3 * RWKV_W]
    z = ps[:, LORA_Z0:GATE_Z0]
    lane = lax.broadcasted_iota(jnp.int32, z.shape, 1)
    zt = jnp.where(lane < 64, jnp.tanh(z), z)
    wa = _dot(zt.astype(BF16), w2)
    w_log = -_softplus(-(w0 + wa[:, :RWKV_W])) - 0.5
    logw = -jnp.exp(w_log)
    a = _sigmoid(a0 + wa[:, RWKV_W:])
    g = _dot(_sigmoid(ps[:, GATE_Z0:SHIFT_W]).astype(BF16), gl_up)
    kk = k * k_k
    kk = kk * lax.rsqrt(_head_sum(kk * kk, bones) + NORM_EPS)
    k2 = k * (1.0 + (a - 1.0) * k_a)
    return r, k2, v, logw, a, g, kk


def _rwkv_postlude(y, r, k2, v, g, r_k, ln_g, ln_b, bones):
    mean = _head_sum(y, bones) * (1.0 / HEAD_DIM)
    yc = y - mean
    var = _head_sum(yc * yc, bones) * (1.0 / HEAD_DIM)
    yn = yc * lax.rsqrt(var + GN_EPS) * ln_g + ln_b
    bonus = _head_sum(r * k2 * r_k, bones) * v
    return (yn + bonus) * g


def _inproj_pool_kernel(x_ref, g1_ref, win_ref, pw_ref, ps_ref, u_ref, p_ref, po_ref, ext_ref):
    tb = pl.program_id(1)
    rows = x_ref.shape[0]

    @pl.when(tb == 0)
    def _():
        ext_ref[0:POOL_HALO, :] = jnp.zeros((POOL_HALO, POOL_W), F32)

    h = _rmsnorm(x_ref[...], g1_ref[...]).astype(BF16)
    proj = _dot(h, win_ref[...])
    u = proj[:, :POOL_W]
    u_ref[...] = u
    p_ref[...] = proj[:, POOL_W:]
    ext_ref[POOL_HALO:POOL_HALO + rows, :] = u

    pos = tb * rows + lax.broadcasted_iota(jnp.int32, (rows, 1), 0)
    outs = []
    for gi, win in enumerate(POOL_WINDOWS):
        cols = slice(gi * POOL_GC, (gi + 1) * POOL_GC)
        acc = u[:, cols]
        for j in range(1, win):
            acc = acc + ext_ref[POOL_HALO - j:POOL_HALO - j + rows, cols]
        cnt = jnp.minimum(win, pos + 1).astype(F32)
        d = acc / cnt - u[:, cols]
        outs.append(_dot(d.astype(BF16), pw_ref[gi]))
    po_ref[...] = (jnp.concatenate(outs, axis=1) * ps_ref[...]).astype(po_ref.dtype)
    ext_ref[0:POOL_HALO, :] = ext_ref[rows:rows + POOL_HALO, :]


def _inproj_pool(x, g1, win, pw, ps):
    b, t, _ = x.shape
    rows = ROW_BLOCK
    const = lambda *shape: pl.BlockSpec(shape, lambda i, j: (0,) * len(shape))
    return pl.pallas_call(
        _inproj_pool_kernel,
        grid=(b, t // rows),
        in_specs=[
            pl.BlockSpec((None, rows, D_MODEL), lambda i, j: (i, j, 0)),
            const(1, D_MODEL),
            const(D_MODEL, IN_W),
            const(len(POOL_WINDOWS), POOL_GC, POOL_GC),
            const(1, POOL_W),
        ],
        out_specs=[
            pl.BlockSpec((None, rows, POOL_W), lambda i, j: (i, j, 0)),
            pl.BlockSpec((None, rows, SHIFT_W), lambda i, j: (i, j, 0)),
            pl.BlockSpec((None, rows, POOL_W), lambda i, j: (i, j, 0)),
        ],
        out_shape=[
            jax.ShapeDtypeStruct((b, t, POOL_W), F32),
            jax.ShapeDtypeStruct((b, t, SHIFT_W), F32),
            jax.ShapeDtypeStruct((b, t, POOL_W), BF16),
        ],
        scratch_shapes=[pltpu.VMEM((rows + POOL_HALO, POOL_W), F32)],
        compiler_params=pltpu.CompilerParams(
            dimension_semantics=("arbitrary", "arbitrary"), vmem_limit_bytes=VMEM_LIMIT),
        name="inproj_pool",
    )(x, g1, win, pw, ps)


def _expand(x_bf, half):
    zero = jnp.zeros((x_bf.shape[0], 128), x_bf.dtype)
    blocks = []
    for h in range(GROUP_HEADS):
        tile = h // 2
        piece = x_bf[:, tile * 128:(tile + 1) * 128] * half[h % 2]
        blocks.append(jnp.concatenate([piece, zero] if tile == 0 else [zero, piece], axis=1))
    return jnp.concatenate(blocks, axis=0)


def _diag_blocks(full, low):
    tiles = []
    for tile in range(GROUP_HEADS // 2):
        cols = slice(tile * 128, (tile + 1) * 128)
        left = full[(2 * tile) * HEAD_DIM:(2 * tile + 1) * HEAD_DIM, cols]
        right = full[(2 * tile + 1) * HEAD_DIM:(2 * tile + 2) * HEAD_DIM, cols]
        tiles.append(jnp.where(low, left, right))
    return jnp.concatenate(tiles, axis=1)


def _chunk_groups(chains, strict, incl, iden, half):
    c = chains[0][1].shape[0]
    each = lambda fn, *lists: [fn(*args) for args in zip(*lists)]
    bf = lambda x: x.astype(BF16)
    blockdiag = lambda x: _expand(bf(x), half)
    rows2 = lambda x, y: jnp.concatenate([x, y], axis=0)
    low = lax.broadcasted_iota(jnp.int32, (HEAD_DIM, 128), 1) < HEAD_DIM
    s0, r, k2, v, aa, bb, w_in, w_ex, w_inv = (list(x) for x in zip(*chains))

    at = each(lambda a_, w_: bf(a_ * w_), aa, w_ex)
    rt = each(lambda r_, w_: bf(r_ * w_), r, w_in)
    bt = each(lambda b_, w_: bf(b_ * w_), bb, w_inv)
    kt = each(lambda k_, w_: bf(k_ * w_), k2, w_inv)
    vb = each(bf, v)
    ar = each(rows2, at, rt)
    e_bk = each(lambda b_, k_: rows2(_expand(b_, half), _expand(k_, half)), bt, kt)
    sc = each(_dot_nt, ar, e_bk)
    l_ab = each(lambda x: x[:c, :GROUP_W] * strict, sc)
    lm_k = each(lambda x: bf(rows2(x[:c, GROUP_W:] * strict, x[c:, GROUP_W:] * incl)), sc)
    m_rb = each(lambda x: bf(x[c:, :GROUP_W] * incl), sc)

    n = each(lambda x: iden + x, l_ab)
    p = each(lambda x: _dot(bf(x), blockdiag(x)), l_ab)
    for _ in range(int(np.log2(c)) - 2):
        out = each(lambda p_, n_: _dot(bf(rows2(p_, n_)), blockdiag(p_)), p, n)
        p = each(lambda x: x[:c], out)
        n = each(lambda n_, x: n_ + x[c:], n, out)
    t_inv = each(lambda n_, p_: bf(n_ + _dot(bf(n_), blockdiag(p_))), n, p)

    lv = each(lambda m_, v_: _dot(m_, _expand(v_, half)), lm_k, vb)
    ps = each(lambda a_, s_: _dot_nt(a_, blockdiag(s_)), ar, s0)
    u = each(lambda t_, p_, l_: _dot(t_, blockdiag(p_[:c] + l_[:c])), t_inv, ps, lv)
    ub = each(bf, u)
    y = each(lambda p_, m_, u_, l_: p_[c:] + _dot(m_, _expand(u_, half)) + l_[c:], ps, m_rb, ub, lv)
    full = each(lambda u_, v_, b_, k_: lax.dot_general(rows2(u_, v_), rows2(b_, k_), (((0,), (0,)), ((), ())),
                                                       preferred_element_type=F32), ub, vb, bt, kt)
    s_new = each(lambda s_, f_, w_: (s_ + _diag_blocks(f_, low)) * w_[c - 1:c, :], s0, full, w_in)
    return y, s_new


def _rwkv_chunk_kernel(p_ref, mu_ref, w0_ref, a0_ref, kk_ref, ka_ref, rk_ref, lng_ref, lnb_ref,
                       w2_ref, glup_ref, bones_ref, strict_ref, incl_ref, iden_ref, half_ref, tri_ref,
                       rw_ref, sfin_ref, s_scr, prev_scr):
    t = pl.program_id(1)
    nb, c, _ = p_ref.shape

    @pl.when(t == 0)
    def _():
        s_scr[...] = jnp.zeros_like(s_scr)
        prev_scr[...] = jnp.zeros_like(prev_scr)

    row = lax.broadcasted_iota(jnp.int32, (c, 1), 0)
    pfs = [p_ref[n] for n in range(nb)]
    prevs = [jnp.where(row == 0, prev_scr[n, 0:1, :], pltpu.roll(pfs[n], 1, 0)) for n in range(nb)]
    bones = bones_ref[...]
    r, k2, v, logw, a, g, kk = _rwkv_prelude(
        jnp.concatenate(pfs, axis=0), jnp.concatenate(prevs, axis=0),
        mu_ref[...], w0_ref[...], a0_ref[...], kk_ref[...], ka_ref[...],
        w2_ref[...], glup_ref[...], bones)
    aa = -kk
    bb = kk * a

    l1 = logw.astype(BF16)
    r1 = logw - l1.astype(F32)
    l2 = r1.astype(BF16)
    l3 = (r1 - l2.astype(F32)).astype(BF16)

    strict = strict_ref[...]
    incl = incl_ref[...]
    iden = iden_ref[...]
    half = half_ref[...]
    tri = tri_ref[...]
    chains = []
    for n in range(nb):
        rows = slice(n * c, (n + 1) * c)
        cl = _dot(tri, jnp.concatenate([l1[rows], l2[rows], l3[rows], jnp.zeros_like(l1[rows])], axis=0))
        w_in = jnp.exp(cl)
        w_ex = jnp.exp(cl - logw[rows])
        w_inv = jnp.exp(-cl)
        for gi in range(N_GROUPS):
            sl = slice(gi * GROUP_W, (gi + 1) * GROUP_W)
            chains.append((s_scr[n, :, sl], r[rows, sl], k2[rows, sl], v[rows, sl], aa[rows, sl], bb[rows, sl],
                           w_in[:, sl], w_ex[:, sl], w_inv[:, sl]))
        prev_scr[n, 0:1, :] = pfs[n][c - 1:c, :]
    y_c, s_c = _chunk_groups(chains, strict, incl, iden, half)
    for n in range(nb):
        for gi in range(N_GROUPS):
            s_scr[n, :, gi * GROUP_W:(gi + 1) * GROUP_W] = s_c[n * N_GROUPS + gi]
    y = jnp.concatenate([jnp.concatenate(y_c[n * N_GROUPS:(n + 1) * N_GROUPS], axis=1) for n in range(nb)], axis=0)
    out = _rwkv_postlude(y, r, k2, v, g, rk_ref[...], lng_ref[...], lnb_ref[...], bones)
    rw_ref[...] = out.reshape(nb, c, RWKV_W).astype(rw_ref.dtype)

    @pl.when(t == pl.num_programs(1) - 1)
    def _():
        sfin_ref[...] = s_scr[...]


def _chunk_constants():
    c = CHUNK
    t = np.arange(c)[:, None]
    s = (np.arange(GROUP_HEADS * c) % c)[None, :]
    strict = (s < t).astype(np.float32)
    incl = (s <= t).astype(np.float32)
    iden = (s == t).astype(np.float32)
    lane_low = np.broadcast_to(np.arange(128)[None, :] < HEAD_DIM, (c, 128))
    half = np.stack([lane_low, ~lane_low]).astype(np.float32)
    tri = (np.arange(c)[None, :] <= np.arange(c)[:, None]).astype(np.float32)
    tri = np.concatenate([tri, tri, tri, np.zeros_like(tri)], axis=1)
    return (jnp.asarray(strict), jnp.asarray(incl), jnp.asarray(iden),
            jnp.asarray(half, dtype=BF16), jnp.asarray(tri, dtype=BF16))


def _head_ones():
    h = np.arange(GROUP_W) // HEAD_DIM
    return jnp.asarray((h[:, None] == h[None, :]).astype(np.float32), dtype=BF16)


def _rwkv_chunked(p, rows_prm, w2, gl_up, bones):
    b, t, _ = p.shape
    c = CHUNK
    nb = SEQ_BLOCK
    consts = _chunk_constants()
    const2 = lambda arr: pl.BlockSpec(arr.shape, lambda i, j: (0,) * arr.ndim)
    operands = list(rows_prm) + [w2, gl_up, bones] + list(consts)
    return pl.pallas_call(
        _rwkv_chunk_kernel,
        grid=(b // nb, t // c),
        in_specs=[pl.BlockSpec((nb, c, SHIFT_W), lambda i, j: (i, j, 0))] + [const2(a) for a in operands],
        out_specs=[
            pl.BlockSpec((nb, c, RWKV_W), lambda i, j: (i, j, 0)),
            pl.BlockSpec((nb, HEAD_DIM, RWKV_W), lambda i, j: (i, 0, 0)),
        ],
        out_shape=[
            jax.ShapeDtypeStruct((b, t, RWKV_W), BF16),
            jax.ShapeDtypeStruct((b, HEAD_DIM, RWKV_W), F32),
        ],
        scratch_shapes=[pltpu.VMEM((nb, HEAD_DIM, RWKV_W), F32), pltpu.VMEM((nb, 8, SHIFT_W), F32)],
        compiler_params=pltpu.CompilerParams(
            dimension_semantics=("arbitrary", "arbitrary"), vmem_limit_bytes=VMEM_LIMIT),
        name="rwkv_chunked",
    )(p, *operands)


def _outproj_mlp_kernel(x_ref, po_ref, rw_ref, wout_ref, g2_ref, wup_ref, wdn_ref, gf_ref, y_ref):
    mix = _dot(po_ref[...], wout_ref[0:POOL_W, :]) + _dot(rw_ref[...], wout_ref[POOL_W:, :])
    x1 = x_ref[...] + mix
    h2 = _rmsnorm(x1, g2_ref[...]).astype(BF16)
    acc = jnp.zeros_like(x1)
    for j in range(D_FF // FF_BLOCK):
        cols = slice(j * FF_BLOCK, (j + 1) * FF_BLOCK)
        hid = jnp.maximum(_dot(h2, wup_ref[:, cols]), 0.0)
        acc = acc + _dot((hid * hid).astype(BF16), wdn_ref[cols, :])
    y_ref[...] = _rmsnorm(x1 + acc, gf_ref[...])


def _outproj_mlp(x, po, rw, wout, g2, wup, wdn, gf):
    n = x.shape[0]
    rows = min(ROW_BLOCK, n)
    const = lambda *shape: pl.BlockSpec(shape, lambda i: (0,) * len(shape))
    return pl.pallas_call(
        _outproj_mlp_kernel,
        grid=(n // rows,),
        in_specs=[
            pl.BlockSpec((rows, D_MODEL), lambda i: (i, 0)),
            pl.BlockSpec((rows, POOL_W), lambda i: (i, 0)),
            pl.BlockSpec((rows, RWKV_W), lambda i: (i, 0)),
            const(D_MODEL, D_MODEL),
            const(1, D_MODEL),
            const(D_MODEL, D_FF),
            const(D_FF, D_MODEL),
            const(1, D_MODEL),
        ],
        out_specs=pl.BlockSpec((rows, D_MODEL), lambda i: (i, 0)),
        out_shape=jax.ShapeDtypeStruct((n, D_MODEL), F32),
        compiler_params=pltpu.CompilerParams(
            dimension_semantics=("arbitrary",), vmem_limit_bytes=VMEM_LIMIT),
        name="outproj_mlp",
    )(x, po, rw, wout, g2, wup, wdn, gf)


def _sample_front_kernel(x_ref, sp_ref, sh_ref, g1_ref, win_ref, pw_ref, ps_ref, mu_ref, w0_ref, a0_ref,
                         kk_ref, ka_ref, w2_ref, glup_ref, bones_ref,
                         u_ref, p_ref, po_ref, tvec_ref, vec_ref):
    h = _rmsnorm(x_ref[...], g1_ref[...]).astype(BF16)
    proj = _dot(h, win_ref[...])
    u = proj[:, :POOL_W]
    pf = proj[:, POOL_W:]
    u_ref[...] = u
    p_ref[...] = pf

    outs = []
    for gi, win in enumerate(POOL_WINDOWS):
        cols = slice(gi * POOL_GC, (gi + 1) * POOL_GC)
        acc = u[:, cols]
        for j in range(1, win):
            acc = acc + sp_ref[POOL_BUF - j, :, cols]
        d = acc * (1.0 / win) - u[:, cols]
        outs.append(_dot(d.astype(BF16), pw_ref[gi]))
    po_ref[...] = (jnp.concatenate(outs, axis=1) * ps_ref[...]).astype(po_ref.dtype)

    bones = bones_ref[...]
    r, k2, v, logw, a, g, kk = _rwkv_prelude(
        pf, sh_ref[...], mu_ref[...], w0_ref[...], a0_ref[...], kk_ref[...], ka_ref[...],
        w2_ref[...], glup_ref[...], bones)
    w = jnp.exp(logw)
    aa = -kk
    bb = kk * a
    q = w * r + aa * _head_sum(bb * r, bones)
    yv = v * _head_sum(k2 * r, bones)
    for i, x in enumerate((aa, q, w, bb, k2, v)):
        tvec_ref[i] = x.T
    for i, x in enumerate((yv, r, k2, v, g)):
        vec_ref[i] = x


def _sample_front(x, sp, sh, g1, win, pw, ps, rows_prm, w2, gl_up, bones):
    n = x.shape[0]
    mu, w0, a0, k_k, k_a = rows_prm
    return pl.pallas_call(
        _sample_front_kernel,
        out_shape=[
            jax.ShapeDtypeStruct((n, POOL_W), F32),
            jax.ShapeDtypeStruct((n, SHIFT_W), F32),
            jax.ShapeDtypeStruct((n, POOL_W), BF16),
            jax.ShapeDtypeStruct((6, RWKV_W, n), F32),
            jax.ShapeDtypeStruct((5, n, RWKV_W), F32),
        ],
        compiler_params=pltpu.CompilerParams(vmem_limit_bytes=VMEM_LIMIT),
        name="sample_front",
    )(x, sp, sh, g1, win, pw, ps, mu, w0, a0, k_k, k_a, w2, gl_up, bones)


def _decode_state_kernel(s_ref, tvec_ref, snew_ref, yt_ref):
    rows = s_ref.shape[0]
    v0 = pl.program_id(1) * rows
    aa = tvec_ref[0]
    q = tvec_ref[1]
    w = tvec_ref[2]
    bb = tvec_ref[3]
    k = tvec_ref[4]
    for i in range(rows):
        s0 = s_ref[i]
        v_row = tvec_ref[5, pl.ds(v0 + i, 1), :]
        sa = jnp.sum(s0 * aa, axis=0, keepdims=True)
        yt_ref[i:i + 1, :] = jnp.sum(s0 * q, axis=0, keepdims=True)
        snew_ref[i] = s0 * w + sa * bb + v_row * k


def _decode_state(s, tvec):
    n = s.shape[-1]
    rows = DEC_ROWS
    return pl.pallas_call(
        _decode_state_kernel,
        grid=(N_HEADS, HEAD_DIM // rows),
        in_specs=[
            pl.BlockSpec((None, rows, HEAD_DIM, n), lambda h, j: (h, j, 0, 0)),
            pl.BlockSpec((6, None, HEAD_DIM, n), lambda h, j: (0, h, 0, 0)),
        ],
        out_specs=[
            pl.BlockSpec((None, rows, HEAD_DIM, n), lambda h, j: (h, j, 0, 0)),
            pl.BlockSpec((None, rows, n), lambda h, j: (h, j, 0)),
        ],
        out_shape=[
            jax.ShapeDtypeStruct(s.shape, F32),
            jax.ShapeDtypeStruct((N_HEADS, HEAD_DIM, n), F32),
        ],
        compiler_params=pltpu.CompilerParams(
            dimension_semantics=("arbitrary", "arbitrary"), vmem_limit_bytes=VMEM_LIMIT),
        name="decode_state",
    )(s, tvec)


def _sample_post_kernel(y0t_ref, vec_ref, rk_ref, lng_ref, lnb_ref, bones_ref, rw_ref):
    y = y0t_ref[...].T + vec_ref[0]
    out = _rwkv_postlude(y, vec_ref[1], vec_ref[2], vec_ref[3], vec_ref[4],
                         rk_ref[...], lng_ref[...], lnb_ref[...], bones_ref[...])
    rw_ref[...] = out.astype(rw_ref.dtype)


def _sample_post(y0t, vec, r_k, ln_g, ln_b, bones):
    n = y0t.shape[1]
    return pl.pallas_call(
        _sample_post_kernel,
        out_shape=jax.ShapeDtypeStruct((n, RWKV_W), BF16),
        compiler_params=pltpu.CompilerParams(vmem_limit_bytes=VMEM_LIMIT),
        name="sample_post",
    )(y0t, vec, r_k, ln_g, ln_b, bones)


def kernel(x_prompt, x_sample, state_wkv, state_shift, state_pool, norm1_g, w_in, shift_mu, pool_w, pool_scale,
           w0, w_lora_up, a0, a_lora_up, g_lora_up, k_k, k_a, r_k, ln_x_g, ln_x_b, w_out, norm2_g, w_up, w_down,
           norm_f_g):
    depth = w_in.shape[0]
    assert depth == 1, "single-layer step"
    b, t, _ = x_prompt.shape
    n = x_sample.shape[0]
    row = lambda a: a.reshape(1, -1).astype(F32)

    g1 = row(norm1_g[0])
    win = w_in[0].astype(BF16)
    pw = pool_w[0].astype(BF16)
    ps = row(pool_scale[0])
    zeros = jnp.zeros((64, RWKV_W), F32)
    w2 = jnp.concatenate([jnp.concatenate([w_lora_up[0], zeros], axis=1),
                          jnp.concatenate([zeros, a_lora_up[0]], axis=1)], axis=0).astype(BF16)
    gl_up = g_lora_up[0].astype(BF16)
    bones = _head_ones()
    pre_rows = (row(shift_mu[0]), row(w0[0]), row(a0[0]), row(k_k[0]), row(k_a[0]))
    post_rows = (row(r_k[0]), row(ln_x_g[0]), row(ln_x_b[0]))
    wout = w_out[0].astype(BF16)
    g2 = row(norm2_g[0])
    wup = w_up[0].astype(BF16)
    wdn = w_down[0].astype(BF16)
    gf = row(norm_f_g)

    u_p, p_p, po_p = _inproj_pool(x_prompt, g1, win, pw, ps)
    rw_p, s_nat = _rwkv_chunked(p_p, pre_rows + post_rows, w2, gl_up, bones)
    y_p = _outproj_mlp(x_prompt.reshape(b * t, D_MODEL), po_p.reshape(b * t, POOL_W),
                       rw_p.reshape(b * t, RWKV_W), wout, g2, wup, wdn, gf).reshape(b, t, D_MODEL)
    wkv_p = jnp.transpose(s_nat.reshape(b, HEAD_DIM, N_HEADS, HEAD_DIM), (0, 2, 1, 3))[None]
    shift_p = p_p[:, t - 1:, :][None]
    pool_p = u_p[:, t - POOL_BUF:, :][None]

    xs = x_sample.reshape(n, D_MODEL)
    sp = state_pool[0]
    u_s, p_s, po_s, tvec, vec = _sample_front(xs, jnp.transpose(sp, (1, 0, 2)), state_shift[0].reshape(n, SHIFT_W),
                                              g1, win, pw, ps, pre_rows, w2, gl_up, bones)
    s_t = jnp.transpose(state_wkv[0], (1, 2, 3, 0))
    s_new_t, y0t = _decode_state(s_t, tvec.reshape(6, N_HEADS, HEAD_DIM, n))
    s_new = jnp.transpose(s_new_t, (3, 0, 1, 2))
    rw_s = _sample_post(y0t.reshape(RWKV_W, n), vec, *post_rows, bones)
    y_s = _outproj_mlp(xs, po_s, rw_s, wout, g2, wup, wdn, gf).reshape(n, 1, D_MODEL)
    shift_s = p_s.reshape(1, n, 1, SHIFT_W)
    pool_s = jnp.concatenate([sp[:, 1:, :], u_s[:, None, :]], axis=1)[None]

    return (y_p, y_s, wkv_p, shift_p, pool_p, s_new[None], shift_s, pool_s)
```

```python
import functools

import jax
import jax.numpy as jnp
import numpy as np
from jax import lax
from jax.experimental import pallas as pl
from jax.experimental.pallas import tpu as pltpu

F32 = jnp.float32
BF16 = jnp.bfloat16

D_MODEL = 1024
POOL_W = 512
RWKV_W = 512
POOL_WINDOWS = (2, 4, 8, 16)
POOL_GC = 128
POOL_BUF = 15
HEAD_DIM = 64
N_HEADS = 8
SHIFT_W = 1792
IN_W = POOL_W + SHIFT_W
D_FF = 4096
LORA_Z0 = 3 * RWKV_W
GATE_Z0 = LORA_Z0 + 128
RMS_EPS = 1e-6
GN_EPS = 64e-5
NORM_EPS = 1e-12
DECAY_SCALE = float(np.exp(-0.5))

CHUNK = 64
SEQ_BLOCK = 4
GROUP_HEADS = 4
GROUP_W = GROUP_HEADS * HEAD_DIM
N_GROUPS = N_HEADS // GROUP_HEADS
POOL_HALO = 16
ROW_BLOCK = 512
FF_BLOCK = 1024
DEC_ROWS = 32
VMEM_LIMIT = 56 * 1024 * 1024


def _dot(a, b):
    return jnp.dot(a, b, preferred_element_type=F32)


def _dot_nt(a, b):
    return lax.dot_general(a, b, (((1,), (1,)), ((), ())), preferred_element_type=F32)


def _rmsnorm(x, g):
    return x * lax.rsqrt(jnp.mean(x * x, axis=-1, keepdims=True) + RMS_EPS) * g


def _sigmoid(x):
    return 1.0 / (1.0 + jnp.exp(-x))


def _head_sum(x, bones):
    parts = [_dot(x[:, g * GROUP_W:(g + 1) * GROUP_W].astype(BF16), bones) for g in range(N_GROUPS)]
    return jnp.concatenate(parts, axis=1)


def _token_shift(pf, prev, mu):
    return pf + (prev - pf) * mu


def _prelude_stages(ctx, w0, a0, k_k, k_a, w2, gl_up, bones):
    def split():
        ps = ctx['ps']
        ctx['r'] = ps[:, 0:RWKV_W]
        ctx['k'] = ps[:, RWKV_W:2 * RWKV_W]
        ctx['v'] = ps[:, 2 * RWKV_W:3 * RWKV_W]
        z = ps[:, LORA_Z0:GATE_Z0]
        lane = lax.broadcasted_iota(jnp.int32, z.shape, 1)
        ctx['zt'] = jnp.where(lane < 64, jnp.tanh(z), z).astype(BF16)
        ctx['zg'] = _sigmoid(ps[:, GATE_Z0:SHIFT_W]).astype(BF16)

    def lora():
        ctx['wa'] = _dot(ctx['zt'], w2)

    def decay():
        wa = ctx['wa']
        ctx['logw'] = -DECAY_SCALE * _sigmoid(w0 + wa[:, :RWKV_W])
        ctx['a'] = _sigmoid(a0 + wa[:, RWKV_W:])

    def gate():
        ctx['g'] = _dot(ctx['zg'], gl_up)

    def key_norm():
        kk = ctx['k'] * k_k
        ctx['kk'] = kk
        ctx['kk_ss'] = _head_sum(kk * kk, bones)

    def keys():
        ctx['kk'] = ctx['kk'] * lax.rsqrt(ctx['kk_ss'] + NORM_EPS)
        ctx['k2'] = ctx['k'] * (1.0 + (ctx['a'] - 1.0) * k_a)

    return [split, lora, decay, gate, key_norm, keys]


def _bonus(r, k2, v, r_k, bones):
    return _head_sum(r * k2 * r_k, bones) * v


def _postlude_stages(ctx, ln_g, ln_b, bones):
    def mean():
        ctx['yc'] = ctx['y'] - _head_sum(ctx['y'], bones) * (1.0 / HEAD_DIM)

    def var():
        ctx['var'] = _head_sum(ctx['yc'] * ctx['yc'], bones) * (1.0 / HEAD_DIM)

    def gate():
        yn = ctx['yc'] * lax.rsqrt(ctx['var'] + GN_EPS) * ln_g + ln_b
        ctx['out'] = (yn + ctx['bonus']) * ctx['g']

    return [mean, var, gate]


def _run(stages):
    for stage in stages:
        stage()


def _merge(main, *sides):
    merged = []
    done = [0] * len(sides)
    for i, stage in enumerate(main):
        merged.append(stage)
        for j, side in enumerate(sides):
            upto = (i + 1) * len(side) // len(main)
            merged.extend(side[done[j]:upto])
            done[j] = upto
    return merged


def _rwkv_prelude(ps, *params):
    ctx = {'ps': ps}
    _run(_prelude_stages(ctx, *params))
    return tuple(ctx[name] for name in ('r', 'k2', 'v', 'logw', 'a', 'g', 'kk'))


def _rwkv_postlude(y, r, k2, v, g, r_k, ln_g, ln_b, bones):
    ctx = {'y': y, 'bonus': _bonus(r, k2, v, r_k, bones), 'g': g}
    _run(_postlude_stages(ctx, ln_g, ln_b, bones))
    return ctx['out']


def _inproj_pool_kernel(x_ref, g1_ref, win_ref, pw_ref, ps_ref, mu_ref, u_ref, sh_ref, po_ref, plast_ref,
                        ext_ref, prow_ref):
    tb = pl.program_id(1)
    rows = x_ref.shape[0]

    @pl.when(tb == 0)
    def _():
        ext_ref[0:POOL_HALO, :] = jnp.zeros((POOL_HALO, POOL_W), F32)
        prow_ref[...] = jnp.zeros_like(prow_ref)

    h = _rmsnorm(x_ref[...], g1_ref[...]).astype(BF16)
    proj = _dot(h, win_ref[...])
    u = proj[:, :POOL_W]
    u_ref[...] = u
    ext_ref[POOL_HALO:POOL_HALO + rows, :] = u

    pf = proj[:, POOL_W:]
    first = lax.broadcasted_iota(jnp.int32, (rows, 1), 0) == 0
    prev = jnp.where(first, prow_ref[0:1, :], pltpu.roll(pf, 1, 0))
    sh_ref[...] = _token_shift(pf, prev, mu_ref[...])
    prow_ref[0:1, :] = pf[rows - 1:rows, :]
    plast_ref[...] = jnp.broadcast_to(pf[rows - 1:rows, :], plast_ref.shape)

    pos = tb * rows + lax.broadcasted_iota(jnp.int32, (rows, 1), 0)
    outs = []
    for gi, win in enumerate(POOL_WINDOWS):
        cols = slice(gi * POOL_GC, (gi + 1) * POOL_GC)
        acc = u[:, cols]
        for j in range(1, win):
            acc = acc + ext_ref[POOL_HALO - j:POOL_HALO - j + rows, cols]
        cnt = jnp.minimum(win, pos + 1).astype(F32)
        d = acc / cnt - u[:, cols]
        outs.append(_dot(d.astype(BF16), pw_ref[gi]))
    po_ref[...] = (jnp.concatenate(outs, axis=1) * ps_ref[...]).astype(po_ref.dtype)
    ext_ref[0:POOL_HALO, :] = ext_ref[rows:rows + POOL_HALO, :]


def _inproj_pool(x, g1, win, pw, ps, mu):
    b, t, _ = x.shape
    rows = ROW_BLOCK
    const = lambda *shape: pl.BlockSpec(shape, lambda i, j: (0,) * len(shape))
    return pl.pallas_call(
        _inproj_pool_kernel,
        grid=(b, t // rows),
        in_specs=[
            pl.BlockSpec((None, rows, D_MODEL), lambda i, j: (i, j, 0)),
            const(1, D_MODEL),
            const(D_MODEL, IN_W),
            const(len(POOL_WINDOWS), POOL_GC, POOL_GC),
            const(1, POOL_W),
            const(1, SHIFT_W),
        ],
        out_specs=[
            pl.BlockSpec((None, rows, POOL_W), lambda i, j: (i, j, 0)),
            pl.BlockSpec((None, rows, SHIFT_W), lambda i, j: (i, j, 0)),
            pl.BlockSpec((None, rows, POOL_W), lambda i, j: (i, j, 0)),
            pl.BlockSpec((None, 8, SHIFT_W), lambda i, j: (i, 0, 0)),
        ],
        out_shape=[
            jax.ShapeDtypeStruct((b, t, POOL_W), F32),
            jax.ShapeDtypeStruct((b, t, SHIFT_W), F32),
            jax.ShapeDtypeStruct((b, t, POOL_W), BF16),
            jax.ShapeDtypeStruct((b, 8, SHIFT_W), F32),
        ],
        scratch_shapes=[pltpu.VMEM((rows + POOL_HALO, POOL_W), F32), pltpu.VMEM((8, SHIFT_W), F32)],
        compiler_params=pltpu.CompilerParams(
            dimension_semantics=("arbitrary", "arbitrary"), vmem_limit_bytes=VMEM_LIMIT),
        name="inproj_pool",
    )(x, g1, win, pw, ps, mu)


def _expand(x_bf, half):
    zero = jnp.zeros((x_bf.shape[0], 128), x_bf.dtype)
    blocks = []
    for h in range(GROUP_HEADS):
        tile = h // 2
        piece = x_bf[:, tile * 128:(tile + 1) * 128] * half[h % 2]
        blocks.append(jnp.concatenate([piece, zero] if tile == 0 else [zero, piece], axis=1))
    return jnp.concatenate(blocks, axis=0)


def _diag_blocks(full, low):
    tiles = []
    for tile in range(GROUP_HEADS // 2):
        cols = slice(tile * 128, (tile + 1) * 128)
        left = full[(2 * tile) * HEAD_DIM:(2 * tile + 1) * HEAD_DIM, cols]
        right = full[(2 * tile + 1) * HEAD_DIM:(2 * tile + 2) * HEAD_DIM, cols]
        tiles.append(jnp.where(low, left, right))
    return jnp.concatenate(tiles, axis=1)


def _chunk_stages(ctx, strict, incl, iden, half):
    c = CHUNK
    each = lambda fn, *names: [fn(*args) for args in zip(*(ctx[name] for name in names))]
    bf = lambda x: x.astype(BF16)
    blockdiag = lambda x: _expand(bf(x), half)
    rows2 = lambda x, y: jnp.concatenate([x, y], axis=0)

    def scores():
        ctx['ar'] = each(rows2, 'at', 'rt')
        e_bk = each(lambda b_, k_: rows2(_expand(b_, half), _expand(k_, half)), 'bt', 'kt')
        ctx['sc'] = [_dot_nt(a_, e_) for a_, e_ in zip(ctx['ar'], e_bk)]

    def masks():
        ctx['lm_k'] = each(lambda x: bf(rows2(x[:c, GROUP_W:] * strict, x[c:, GROUP_W:] * incl)), 'sc')
        ctx['m_rb'] = each(lambda x: bf(x[c:, :GROUP_W] * incl), 'sc')
        ctx['p'] = each(lambda x: x[:c, :GROUP_W] * strict, 'sc')
        ctx['n'] = each(lambda x: iden + x, 'p')
        del ctx['sc']

    def square():
        ctx['p'] = each(lambda x: _dot(bf(x), blockdiag(x)), 'p')

    def double():
        out = each(lambda p_, n_: _dot(bf(rows2(p_, n_)), blockdiag(p_)), 'p', 'n')
        ctx['p'] = [x[:c] for x in out]
        ctx['n'] = [n_ + x[c:] for n_, x in zip(ctx['n'], out)]

    def inverse():
        ctx['t_inv'] = each(lambda n_, p_: bf(n_ + _dot(bf(n_), blockdiag(p_))), 'n', 'p')

    def values():
        ctx['lv'] = each(lambda m_, v_: _dot(m_, _expand(v_, half)), 'lm_k', 'vb')

    def state_in():
        ctx['ps'] = each(lambda a_, s_: _dot_nt(a_, blockdiag(s_)), 'ar', 's0')

    def solve():
        ctx['ub'] = each(lambda t_, p_, l_: bf(_dot(t_, blockdiag(p_[:c] + l_[:c]))), 't_inv', 'ps', 'lv')

    def outputs():
        ctx['y'] = each(lambda p_, m_, u_, l_: p_[c:] + _dot(m_, _expand(u_, half)) + l_[c:], 'ps', 'm_rb', 'ub', 'lv')

    def state_out():
        low = lax.broadcasted_iota(jnp.int32, (HEAD_DIM, 128), 1) < HEAD_DIM
        full = each(lambda u_, v_, b_, k_: lax.dot_general(rows2(u_, v_), rows2(b_, k_), (((0,), (0,)), ((), ())),
                                                           preferred_element_type=F32), 'ub', 'vb', 'bt', 'kt')
        ctx['s_new'] = [(s_ + _diag_blocks(f_, low)) * w_ for s_, f_, w_ in zip(ctx['s0'], full, ctx['w_last'])]

    levels = int(np.log2(c))
    return ([scores, masks, square] + [double] * (levels - 2)
            + [inverse, values, state_in, solve, outputs, state_out])


def _rwkv_chunk_kernel(sh_ref, w0_ref, a0_ref, kk_ref, ka_ref, rk_ref, lng_ref, lnb_ref,
                       w2_ref, glup_ref, bones_ref, strict_ref, incl_ref, iden_ref, half_ref, tri_ref,
                       rw_ref, sfin_ref, s_scr):
    t = pl.program_id(1)
    nb, c, _ = sh_ref.shape
    seqs = range(nb)

    @pl.when(t == 0)
    def _():
        s_scr[...] = jnp.zeros_like(s_scr)

    bones = bones_ref[...]
    pre_params = (w0_ref[...], a0_ref[...], kk_ref[...], ka_ref[...], w2_ref[...], glup_ref[...], bones)
    chain_consts = (strict_ref[...], incl_ref[...], iden_ref[...], half_ref[...])
    tri = tri_ref[...]
    groups = [slice(gi * GROUP_W, (gi + 1) * GROUP_W) for gi in range(N_GROUPS)]
    rows = [slice(n * c, (n + 1) * c) for n in seqs]
    ctx = {'ps': jnp.concatenate([sh_ref[n] for n in seqs], axis=0)}

    def cumulative_decay():
        logw = ctx['logw']
        hi = logw.astype(BF16)
        lo = (logw - hi.astype(F32)).astype(BF16)
        ctx['cl'] = [_dot(tri, jnp.concatenate([hi[rs], lo[rs]], axis=0)) for rs in rows]

    def scale():
        aa = -ctx['kk']
        bb = ctx['kk'] * ctx['a']
        chain = {name: [] for name in ('at', 'rt', 'bt', 'kt', 'vb', 'w_last')}
        for rs, cl in zip(rows, ctx['cl']):
            w_in = jnp.exp(cl)
            w_ex = jnp.exp(cl - ctx['logw'][rs])
            w_inv = jnp.exp(-cl)
            for sl in groups:
                chain['at'].append((aa[rs, sl] * w_ex[:, sl]).astype(BF16))
                chain['rt'].append((ctx['r'][rs, sl] * w_in[:, sl]).astype(BF16))
                chain['bt'].append((bb[rs, sl] * w_inv[:, sl]).astype(BF16))
                chain['kt'].append((ctx['k2'][rs, sl] * w_inv[:, sl]).astype(BF16))
                chain['vb'].append(ctx['v'][rs, sl].astype(BF16))
                chain['w_last'].append(w_in[c - 1:c, sl])
        ctx.update(chain)
        ctx['s0'] = [s_scr[n, :, sl] for n in seqs for sl in groups]

    def store_state():
        for n in seqs:
            for gi, sl in enumerate(groups):
                s_scr[n, :, sl] = ctx['s_new'][n * N_GROUPS + gi]
        ctx['y'] = jnp.concatenate([jnp.concatenate(ctx['y'][n * N_GROUPS:(n + 1) * N_GROUPS], axis=1)
                                    for n in seqs], axis=0)

    def bonus():
        ctx['bonus'] = _bonus(ctx['r'], ctx['k2'], ctx['v'], rk_ref[...], bones)

    def store_out():
        rw_ref[...] = ctx['out'].reshape(nb, c, RWKV_W).astype(rw_ref.dtype)

    _run(_prelude_stages(ctx, *pre_params) + [cumulative_decay, scale]
         + _chunk_stages(ctx, *chain_consts) + [store_state, bonus]
         + _postlude_stages(ctx, lng_ref[...], lnb_ref[...], bones) + [store_out])

    @pl.when(t == pl.num_programs(1) - 1)
    def _():
        sfin_ref[...] = s_scr[...]


def _chunk_constants():
    c = CHUNK
    t = np.arange(c)[:, None]
    s = (np.arange(GROUP_HEADS * c) % c)[None, :]
    strict = (s < t).astype(np.float32)
    incl = (s <= t).astype(np.float32)
    iden = (s == t).astype(np.float32)
    lane_low = np.broadcast_to(np.arange(128)[None, :] < HEAD_DIM, (c, 128))
    half = np.stack([lane_low, ~lane_low]).astype(np.float32)
    tri = (np.arange(c)[None, :] <= np.arange(c)[:, None]).astype(np.float32)
    tri = np.concatenate([tri, tri], axis=1)
    return (jnp.asarray(strict), jnp.asarray(incl), jnp.asarray(iden),
            jnp.asarray(half, dtype=BF16), jnp.asarray(tri, dtype=BF16))


def _head_ones():
    h = np.arange(GROUP_W) // HEAD_DIM
    return jnp.asarray((h[:, None] == h[None, :]).astype(np.float32), dtype=BF16)


def _rwkv_chunked(p, rows_prm, w2, gl_up, bones):
    b, t, _ = p.shape
    c = CHUNK
    nb = SEQ_BLOCK
    consts = _chunk_constants()
    const2 = lambda arr: pl.BlockSpec(arr.shape, lambda i, j: (0,) * arr.ndim)
    operands = list(rows_prm) + [w2, gl_up, bones] + list(consts)
    return pl.pallas_call(
        _rwkv_chunk_kernel,
        grid=(b // nb, t // c),
        in_specs=[pl.BlockSpec((nb, c, SHIFT_W), lambda i, j: (i, j, 0))] + [const2(a) for a in operands],
        out_specs=[
            pl.BlockSpec((nb, c, RWKV_W), lambda i, j: (i, j, 0)),
            pl.BlockSpec((nb, HEAD_DIM, RWKV_W), lambda i, j: (i, 0, 0)),
        ],
        out_shape=[
            jax.ShapeDtypeStruct((b, t, RWKV_W), BF16),
            jax.ShapeDtypeStruct((b, HEAD_DIM, RWKV_W), F32),
        ],
        scratch_shapes=[pltpu.VMEM((nb, HEAD_DIM, RWKV_W), F32)],
        compiler_params=pltpu.CompilerParams(
            dimension_semantics=("arbitrary", "arbitrary"), vmem_limit_bytes=VMEM_LIMIT),
        name="rwkv_chunked",
    )(p, *operands)


def _outproj_mlp_kernel(x_ref, po_ref, rw_ref, wout_ref, g2_ref, wup_ref, wdn_ref, gf_ref, y_ref):
    mix = _dot(po_ref[...], wout_ref[0:POOL_W, :]) + _dot(rw_ref[...], wout_ref[POOL_W:, :])
    x1 = x_ref[...] + mix
    h2 = _rmsnorm(x1, g2_ref[...]).astype(BF16)
    acc = jnp.zeros_like(x1)
    for j in range(D_FF // FF_BLOCK):
        cols = slice(j * FF_BLOCK, (j + 1) * FF_BLOCK)
        hid = jnp.maximum(_dot(h2, wup_ref[:, cols]), 0.0)
        acc = acc + _dot((hid * hid).astype(BF16), wdn_ref[cols, :])
    y_ref[...] = _rmsnorm(x1 + acc, gf_ref[...])


def _outproj_mlp(x, po, rw, wout, g2, wup, wdn, gf):
    n = x.shape[0]
    rows = min(ROW_BLOCK, n)
    const = lambda *shape: pl.BlockSpec(shape, lambda i: (0,) * len(shape))
    return pl.pallas_call(
        _outproj_mlp_kernel,
        grid=(n // rows,),
        in_specs=[
            pl.BlockSpec((rows, D_MODEL), lambda i: (i, 0)),
            pl.BlockSpec((rows, POOL_W), lambda i: (i, 0)),
            pl.BlockSpec((rows, RWKV_W), lambda i: (i, 0)),
            const(D_MODEL, D_MODEL),
            const(1, D_MODEL),
            const(D_MODEL, D_FF),
            const(D_FF, D_MODEL),
            const(1, D_MODEL),
        ],
        out_specs=pl.BlockSpec((rows, D_MODEL), lambda i: (i, 0)),
        out_shape=jax.ShapeDtypeStruct((n, D_MODEL), F32),
        compiler_params=pltpu.CompilerParams(
            dimension_semantics=("arbitrary",), vmem_limit_bytes=VMEM_LIMIT),
        name="outproj_mlp",
    )(x, po, rw, wout, g2, wup, wdn, gf)


def _sample_front_kernel(x_ref, sp_ref, sh_ref, g1_ref, win_ref, pw_ref, ps_ref, mu_ref, w0_ref, a0_ref,
                         kk_ref, ka_ref, w2_ref, glup_ref, bones_ref,
                         u_ref, p_ref, po_ref, tvec_ref, vec_ref):
    h = _rmsnorm(x_ref[...], g1_ref[...]).astype(BF16)
    proj = _dot(h, win_ref[...])
    u = proj[:, :POOL_W]
    pf = proj[:, POOL_W:]
    u_ref[...] = u
    p_ref[...] = pf

    outs = []
    for gi, win in enumerate(POOL_WINDOWS):
        cols = slice(gi * POOL_GC, (gi + 1) * POOL_GC)
        acc = u[:, cols]
        for j in range(1, win):
            acc = acc + sp_ref[POOL_BUF - j, :, cols]
        d = acc * (1.0 / win) - u[:, cols]
        outs.append(_dot(d.astype(BF16), pw_ref[gi]))
    po_ref[...] = (jnp.concatenate(outs, axis=1) * ps_ref[...]).astype(po_ref.dtype)

    bones = bones_ref[...]
    r, k2, v, logw, a, g, kk = _rwkv_prelude(
        _token_shift(pf, sh_ref[...], mu_ref[...]), w0_ref[...], a0_ref[...], kk_ref[...], ka_ref[...],
        w2_ref[...], glup_ref[...], bones)
    w = jnp.exp(logw)
    aa = -kk
    bb = kk * a
    q = w * r + aa * _head_sum(bb * r, bones)
    yv = v * _head_sum(k2 * r, bones)
    for i, x in enumerate((aa, q, w, bb, k2, v)):
        tvec_ref[i] = x.T
    for i, x in enumerate((yv, r, k2, v, g)):
        vec_ref[i] = x


def _sample_front(x, sp, sh, g1, win, pw, ps, rows_prm, w2, gl_up, bones):
    n = x.shape[0]
    mu, w0, a0, k_k, k_a = rows_prm
    return pl.pallas_call(
        _sample_front_kernel,
        out_shape=[
            jax.ShapeDtypeStruct((n, POOL_W), F32),
            jax.ShapeDtypeStruct((n, SHIFT_W), F32),
            jax.ShapeDtypeStruct((n, POOL_W), BF16),
            jax.ShapeDtypeStruct((6, RWKV_W, n), F32),
            jax.ShapeDtypeStruct((5, n, RWKV_W), F32),
        ],
        compiler_params=pltpu.CompilerParams(vmem_limit_bytes=VMEM_LIMIT),
        name="sample_front",
    )(x, sp, sh, g1, win, pw, ps, mu, w0, a0, k_k, k_a, w2, gl_up, bones)


def _decode_state_kernel(s_ref, tvec_ref, snew_ref, yt_ref):
    rows = s_ref.shape[0]
    v0 = pl.program_id(1) * rows
    aa = tvec_ref[0]
    q = tvec_ref[1]
    w = tvec_ref[2]
    bb = tvec_ref[3]
    k = tvec_ref[4]
    for i in range(rows):
        s0 = s_ref[i]
        v_row = tvec_ref[5, pl.ds(v0 + i, 1), :]
        sa = jnp.sum(s0 * aa, axis=0, keepdims=True)
        yt_ref[i:i + 1, :] = jnp.sum(s0 * q, axis=0, keepdims=True)
        snew_ref[i] = s0 * w + sa * bb + v_row * k


def _decode_state(s, tvec):
    n = s.shape[-1]
    rows = DEC_ROWS
    return pl.pallas_call(
        _decode_state_kernel,
        grid=(N_HEADS, HEAD_DIM // rows),
        in_specs=[
            pl.BlockSpec((None, rows, HEAD_DIM, n), lambda h, j: (h, j, 0, 0)),
            pl.BlockSpec((6, None, HEAD_DIM, n), lambda h, j: (0, h, 0, 0)),
        ],
        out_specs=[
            pl.BlockSpec((None, rows, HEAD_DIM, n), lambda h, j: (h, j, 0, 0)),
            pl.BlockSpec((None, rows, n), lambda h, j: (h, j, 0)),
        ],
        out_shape=[
            jax.ShapeDtypeStruct(s.shape, F32),
            jax.ShapeDtypeStruct((N_HEADS, HEAD_DIM, n), F32),
        ],
        compiler_params=pltpu.CompilerParams(
            dimension_semantics=("arbitrary", "arbitrary"), vmem_limit_bytes=VMEM_LIMIT),
        name="decode_state",
    )(s, tvec)


def _sample_post_kernel(y0t_ref, vec_ref, rk_ref, lng_ref, lnb_ref, bones_ref, rw_ref):
    y = y0t_ref[...].T + vec_ref[0]
    out = _rwkv_postlude(y, vec_ref[1], vec_ref[2], vec_ref[3], vec_ref[4],
                         rk_ref[...], lng_ref[...], lnb_ref[...], bones_ref[...])
    rw_ref[...] = out.astype(rw_ref.dtype)


def _sample_post(y0t, vec, r_k, ln_g, ln_b, bones):
    n = y0t.shape[1]
    return pl.pallas_call(
        _sample_post_kernel,
        out_shape=jax.ShapeDtypeStruct((n, RWKV_W), BF16),
        compiler_params=pltpu.CompilerParams(vmem_limit_bytes=VMEM_LIMIT),
        name="sample_post",
    )(y0t, vec, r_k, ln_g, ln_b, bones)


def kernel(x_prompt, x_sample, state_wkv, state_shift, state_pool, norm1_g, w_in, shift_mu, pool_w, pool_scale,
           w0, w_lora_up, a0, a_lora_up, g_lora_up, k_k, k_a, r_k, ln_x_g, ln_x_b, w_out, norm2_g, w_up, w_down,
           norm_f_g):
    depth = w_in.shape[0]
    assert depth == 1, "single-layer step"
    b, t, _ = x_prompt.shape
    n = x_sample.shape[0]
    row = lambda a: a.reshape(1, -1).astype(F32)

    g1 = row(norm1_g[0])
    win = w_in[0].astype(BF16)
    pw = pool_w[0].astype(BF16)
    ps = row(pool_scale[0])
    zeros = jnp.zeros((64, RWKV_W), F32)
    w2 = jnp.concatenate([jnp.concatenate([w_lora_up[0], zeros], axis=1),
                          jnp.concatenate([zeros, a_lora_up[0]], axis=1)], axis=0).astype(BF16)
    gl_up = g_lora_up[0].astype(BF16)
    bones = _head_ones()
    mu = row(shift_mu[0])
    pre_rows = (row(w0[0]), row(a0[0]), row(k_k[0]), row(k_a[0]))
    post_rows = (row(r_k[0]), row(ln_x_g[0]), row(ln_x_b[0]))
    wout = w_out[0].astype(BF16)
    g2 = row(norm2_g[0])
    wup = w_up[0].astype(BF16)
    wdn = w_down[0].astype(BF16)
    gf = row(norm_f_g)

    u_p, sh_p, po_p, plast = _inproj_pool(x_prompt, g1, win, pw, ps, mu)
    rw_p, s_nat = _rwkv_chunked(sh_p, pre_rows + post_rows, w2, gl_up, bones)
    y_p = _outproj_mlp(x_prompt.reshape(b * t, D_MODEL), po_p.reshape(b * t, POOL_W),
                       rw_p.reshape(b * t, RWKV_W), wout, g2, wup, wdn, gf).reshape(b, t, D_MODEL)
    wkv_p = jnp.transpose(s_nat.reshape(b, HEAD_DIM, N_HEADS, HEAD_DIM), (0, 2, 1, 3))[None]
    shift_p = plast[:, 0:1, :][None]
    pool_p = u_p[:, t - POOL_BUF:, :][None]

    xs = x_sample.reshape(n, D_MODEL)
    sp = state_pool[0]
    u_s, p_s, po_s, tvec, vec = _sample_front(xs, jnp.transpose(sp, (1, 0, 2)), state_shift[0].reshape(n, SHIFT_W),
                                              g1, win, pw, ps, (mu,) + pre_rows, w2, gl_up, bones)
    s_t = jnp.transpose(state_wkv[0], (1, 2, 3, 0))
    s_new_t, y0t = _decode_state(s_t, tvec.reshape(6, N_HEADS, HEAD_DIM, n))
    s_new = jnp.transpose(s_new_t, (3, 0, 1, 2))
    rw_s = _sample_post(y0t.reshape(RWKV_W, n), vec, *post_rows, bones)
    y_s = _outproj_mlp(xs, po_s, rw_s, wout, g2, wup, wdn, gf).reshape(n, 1, D_MODEL)
    shift_s = p_s.reshape(1, n, 1, SHIFT_W)
    pool_s = jnp.concatenate([sp[:, 1:, :], u_s[:, None, :]], axis=1)[None]

    return (y_p, y_s, wkv_p, shift_p, pool_p, s_new[None], shift_s, pool_s)
```

```python
import jax
import jax.numpy as jnp
import numpy as np
from jax import lax
from jax.experimental import pallas as pl
from jax.experimental.pallas import tpu as pltpu

F32 = jnp.float32
BF16 = jnp.bfloat16

D_MODEL = 1024
POOL_W = 512
RWKV_W = 512
POOL_WINDOWS = (2, 4, 8, 16)
POOL_GC = 128
POOL_BUF = 15
HEAD_DIM = 64
N_HEADS = 8
SHIFT_W = 1792
IN_W = POOL_W + SHIFT_W
D_FF = 4096
LORA_Z0 = 3 * RWKV_W
GATE_Z0 = LORA_Z0 + 128
RMS_EPS = 1e-6
GN_EPS = 64e-5
NORM_EPS = 1e-12
DECAY_SCALE = float(np.exp(-0.5))

LANES = 128
CHUNK = 64
SEQ_BLOCK = 8
GROUP_HEADS = 4
GROUP_W = GROUP_HEADS * HEAD_DIM
N_GROUPS = N_HEADS // GROUP_HEADS
POOL_HALO = 16
ROW_BLOCK = 512
MLP_ROW_BLOCK = 1024
FF_BLOCK = 1024
DEC_ROWS = 32
VMEM_LIMIT = 56 * 1024 * 1024


def _dot(a, b):
    return jnp.dot(a, b, preferred_element_type=F32)


def _dot_nt(a, b):
    return lax.dot_general(a, b, (((1,), (1,)), ((), ())), preferred_element_type=F32)


def _rmsnorm(x, g):
    return x * lax.rsqrt(jnp.mean(x * x, axis=-1, keepdims=True) + RMS_EPS) * g


def _sigmoid(x):
    return 1.0 / (1.0 + jnp.exp(-x))


def _head_sum(x, bones):
    parts = [_dot(x[:, g * GROUP_W:(g + 1) * GROUP_W].astype(BF16), bones) for g in range(N_GROUPS)]
    return jnp.concatenate(parts, axis=1)


def _token_shift(pf, prev, mu):
    return pf + (prev - pf) * mu


def _prelude_stages(ctx, w0, a0, k_k, k_a, w2, gl_up, bones):
    def split():
        ps = ctx['ps']
        ctx['r'] = ps[:, 0:RWKV_W]
        ctx['k'] = ps[:, RWKV_W:2 * RWKV_W]
        ctx['v'] = ps[:, 2 * RWKV_W:3 * RWKV_W]
        z = ps[:, LORA_Z0:GATE_Z0]
        lane = lax.broadcasted_iota(jnp.int32, z.shape, 1)
        ctx['zt'] = jnp.where(lane < 64, jnp.tanh(z), z).astype(BF16)
        ctx['zg'] = _sigmoid(ps[:, GATE_Z0:SHIFT_W]).astype(BF16)

    def lora():
        ctx['wa'] = _dot(ctx['zt'], w2)

    def decay():
        wa = ctx['wa']
        ctx['logw'] = -DECAY_SCALE * _sigmoid(w0 + wa[:, :RWKV_W])
        ctx['a'] = _sigmoid(a0 + wa[:, RWKV_W:])

    def gate():
        ctx['g'] = _dot(ctx['zg'], gl_up)

    def key_norm():
        kk = ctx['k'] * k_k
        ctx['kk'] = kk
        ctx['kk_ss'] = _head_sum(kk * kk, bones)

    def keys():
        ctx['kk'] = ctx['kk'] * lax.rsqrt(ctx['kk_ss'] + NORM_EPS)
        ctx['k2'] = ctx['k'] * (1.0 + (ctx['a'] - 1.0) * k_a)

    return [split, lora, decay, gate, key_norm, keys]


def _bonus(r, k2, v, r_k, bones):
    return _head_sum(r * k2 * r_k, bones) * v


def _postlude_stages(ctx, ln_g, ln_b, bones):
    def mean():
        ctx['yc'] = ctx['y'] - _head_sum(ctx['y'], bones) * (1.0 / HEAD_DIM)

    def var():
        ctx['var'] = _head_sum(ctx['yc'] * ctx['yc'], bones) * (1.0 / HEAD_DIM)

    def gate():
        yn = ctx['yc'] * lax.rsqrt(ctx['var'] + GN_EPS) * ln_g + ln_b
        ctx['out'] = (yn + ctx['bonus']) * ctx['g']

    return [mean, var, gate]


def _run(stages):
    for stage in stages:
        stage()


def _rwkv_prelude(ps, *params):
    ctx = {'ps': ps}
    _run(_prelude_stages(ctx, *params))
    return tuple(ctx[name] for name in ('r', 'k2', 'v', 'logw', 'a', 'g', 'kk'))


def _rwkv_postlude(y, r, k2, v, g, r_k, ln_g, ln_b, bones):
    ctx = {'y': y, 'bonus': _bonus(r, k2, v, r_k, bones), 'g': g}
    _run(_postlude_stages(ctx, ln_g, ln_b, bones))
    return ctx['out']


def _inproj_pool_kernel(x_ref, g1_ref, win_ref, pw_ref, ps_ref, mu_ref, sh_ref, po_ref, plast_ref, ulast_ref,
                        ext_ref, prow_ref):
    tb = pl.program_id(1)
    rows = x_ref.shape[0]

    @pl.when(tb == 0)
    def _():
        ext_ref[0:POOL_HALO, :] = jnp.zeros((POOL_HALO, POOL_W), F32)
        prow_ref[...] = jnp.zeros_like(prow_ref)

    h = _rmsnorm(x_ref[...], g1_ref[...]).astype(BF16)
    proj = _dot(h, win_ref[...])
    u = proj[:, :POOL_W]
    ext_ref[POOL_HALO:POOL_HALO + rows, :] = u
    ulast_ref[...] = u[rows - POOL_HALO:, :]

    pf = proj[:, POOL_W:]
    first = lax.broadcasted_iota(jnp.int32, (rows, 1), 0) == 0
    prev = jnp.where(first, prow_ref[0:1, :], pltpu.roll(pf, 1, 0))
    sh_ref[...] = _token_shift(pf, prev, mu_ref[...])
    prow_ref[0:1, :] = pf[rows - 1:rows, :]
    plast_ref[...] = jnp.broadcast_to(pf[rows - 1:rows, :], plast_ref.shape)

    pos = tb * rows + lax.broadcasted_iota(jnp.int32, (rows, 1), 0)
    outs = []
    for gi, win in enumerate(POOL_WINDOWS):
        cols = slice(gi * POOL_GC, (gi + 1) * POOL_GC)
        acc = u[:, cols]
        for j in range(1, win):
            acc = acc + ext_ref[POOL_HALO - j:POOL_HALO - j + rows, cols]
        cnt = jnp.minimum(win, pos + 1).astype(F32)
        d = acc / cnt - u[:, cols]
        outs.append(_dot(d.astype(BF16), pw_ref[gi]))
    po_ref[...] = (jnp.concatenate(outs, axis=1) * ps_ref[...]).astype(po_ref.dtype)
    ext_ref[0:POOL_HALO, :] = ext_ref[rows:rows + POOL_HALO, :]


def _inproj_pool(x, g1, win, pw, ps, mu):
    b, t, _ = x.shape
    rows = ROW_BLOCK
    const = lambda *shape: pl.BlockSpec(shape, lambda i, j: (0,) * len(shape))
    return pl.pallas_call(
        _inproj_pool_kernel,
        grid=(b, t // rows),
        in_specs=[
            pl.BlockSpec((None, rows, D_MODEL), lambda i, j: (i, j, 0)),
            const(1, D_MODEL),
            const(D_MODEL, IN_W),
            const(len(POOL_WINDOWS), POOL_GC, POOL_GC),
            const(1, POOL_W),
            const(1, SHIFT_W),
        ],
        out_specs=[
            pl.BlockSpec((None, rows, SHIFT_W), lambda i, j: (i, j, 0)),
            pl.BlockSpec((None, rows, POOL_W), lambda i, j: (i, j, 0)),
            pl.BlockSpec((None, 8, SHIFT_W), lambda i, j: (i, 0, 0)),
            pl.BlockSpec((None, POOL_HALO, POOL_W), lambda i, j: (i, 0, 0)),
        ],
        out_shape=[
            jax.ShapeDtypeStruct((b, t, SHIFT_W), F32),
            jax.ShapeDtypeStruct((b, t, POOL_W), BF16),
            jax.ShapeDtypeStruct((b, 8, SHIFT_W), F32),
            jax.ShapeDtypeStruct((b, POOL_HALO, POOL_W), F32),
        ],
        scratch_shapes=[pltpu.VMEM((rows + POOL_HALO, POOL_W), F32), pltpu.VMEM((8, SHIFT_W), F32)],
        compiler_params=pltpu.CompilerParams(
            dimension_semantics=("arbitrary", "arbitrary"), vmem_limit_bytes=VMEM_LIMIT),
        name="inproj_pool",
    )(x, g1, win, pw, ps, mu)


def _expand(x_bf, half):
    zero = jnp.zeros((x_bf.shape[0], LANES), x_bf.dtype)
    blocks = []
    for h in range(GROUP_HEADS):
        tile = h // 2
        piece = x_bf[:, tile * LANES:(tile + 1) * LANES] * half[h % 2]
        blocks.append(jnp.concatenate([piece, zero] if tile == 0 else [zero, piece], axis=1))
    return jnp.concatenate(blocks, axis=0)


def _diag_blocks(full, low):
    tiles = []
    for tile in range(GROUP_HEADS // 2):
        cols = slice(tile * LANES, (tile + 1) * LANES)
        left = full[(2 * tile) * HEAD_DIM:(2 * tile + 1) * HEAD_DIM, cols]
        right = full[(2 * tile + 1) * HEAD_DIM:(2 * tile + 2) * HEAD_DIM, cols]
        tiles.append(jnp.where(low, left, right))
    return jnp.concatenate(tiles, axis=1)


def _chunk_stages(ctx, strict, incl, iden, half):
    c = CHUNK
    each = lambda fn, *names: [fn(*args) for args in zip(*(ctx[name] for name in names))]
    bf = lambda x: x.astype(BF16)
    blockdiag = lambda x: _expand(bf(x), half)
    rows2 = lambda x, y: jnp.concatenate([x, y], axis=0)

    def scores():
        ctx['ar'] = each(rows2, 'at', 'rt')
        e_bk = each(lambda b_, k_: rows2(_expand(b_, half), _expand(k_, half)), 'bt', 'kt')
        ctx['sc'] = [_dot_nt(a_, e_) for a_, e_ in zip(ctx['ar'], e_bk)]

    def masks():
        ctx['lm_k'] = each(lambda x: bf(rows2(x[:c, GROUP_W:] * strict, x[c:, GROUP_W:] * incl)), 'sc')
        ctx['m_rb'] = each(lambda x: bf(x[c:, :GROUP_W] * incl), 'sc')
        ctx['p'] = each(lambda x: x[:c, :GROUP_W] * strict, 'sc')
        ctx['n'] = each(lambda x: iden + x, 'p')
        del ctx['sc']

    def square():
        ctx['p'] = each(lambda x: _dot(bf(x), blockdiag(x)), 'p')

    def double():
        out = each(lambda p_, n_: _dot(bf(rows2(p_, n_)), blockdiag(p_)), 'p', 'n')
        ctx['p'] = [x[:c] for x in out]
        ctx['n'] = [n_ + x[c:] for n_, x in zip(ctx['n'], out)]

    def inverse():
        ctx['t_inv'] = each(lambda n_, p_: bf(n_ + _dot(bf(n_), blockdiag(p_))), 'n', 'p')

    def values():
        ctx['lv'] = each(lambda m_, v_: _dot(m_, _expand(v_, half)), 'lm_k', 'vb')

    def state_in():
        ctx['ps'] = each(lambda a_, s_: _dot_nt(a_, blockdiag(s_)), 'ar', 's0')

    def solve():
        ctx['ub'] = each(lambda t_, p_, l_: bf(_dot(t_, blockdiag(p_[:c] + l_[:c]))), 't_inv', 'ps', 'lv')

    def outputs():
        ctx['y'] = each(lambda p_, m_, u_, l_: p_[c:] + _dot(m_, _expand(u_, half)) + l_[c:], 'ps', 'm_rb', 'ub', 'lv')

    def state_out():
        low = lax.broadcasted_iota(jnp.int32, (HEAD_DIM, LANES), 1) < HEAD_DIM
        full = each(lambda u_, v_, b_, k_: lax.dot_general(rows2(u_, v_), rows2(b_, k_), (((0,), (0,)), ((), ())),
                                                           preferred_element_type=F32), 'ub', 'vb', 'bt', 'kt')
        ctx['s_new'] = [(s_ + _diag_blocks(f_, low)) * w_ for s_, f_, w_ in zip(ctx['s0'], full, ctx['w_last'])]

    levels = int(np.log2(c))
    return ([scores, masks, square] + [double] * (levels - 2)
            + [inverse, values, state_in, solve, outputs, state_out])


def _rwkv_chunk_kernel(sh_ref, w0_ref, a0_ref, kk_ref, ka_ref, rk_ref, lng_ref, lnb_ref,
                       w2_ref, glup_ref, bones_ref, strict_ref, incl_ref, iden_ref, half_ref, tri_ref,
                       rw_ref, sfin_ref, s_scr):
    t = pl.program_id(1)
    nb, c, _ = sh_ref.shape
    seqs = range(nb)

    @pl.when(t == 0)
    def _():
        s_scr[...] = jnp.zeros_like(s_scr)

    bones = bones_ref[...]
    pre_params = (w0_ref[...], a0_ref[...], kk_ref[...], ka_ref[...], w2_ref[...], glup_ref[...], bones)
    chain_consts = (strict_ref[...], incl_ref[...], iden_ref[...], half_ref[...])
    tri = tri_ref[...]
    groups = [slice(gi * GROUP_W, (gi + 1) * GROUP_W) for gi in range(N_GROUPS)]
    rows = [slice(n * c, (n + 1) * c) for n in seqs]
    ctx = {'ps': jnp.concatenate([sh_ref[n] for n in seqs], axis=0)}

    def cumulative_decay():
        logw = ctx['logw']
        hi = logw.astype(BF16)
        lo = (logw - hi.astype(F32)).astype(BF16)
        ctx['cl'] = [_dot(tri, jnp.concatenate([hi[rs], lo[rs]], axis=0)) for rs in rows]

    def scale():
        aa = -ctx['kk']
        bb = ctx['kk'] * ctx['a']
        chain = {name: [] for name in ('at', 'rt', 'bt', 'kt', 'vb', 'w_last')}
        for rs, cl in zip(rows, ctx['cl']):
            w_in = jnp.exp(cl)
            w_ex = jnp.exp(cl - ctx['logw'][rs])
            w_inv = jnp.exp(-cl)
            for sl in groups:
                chain['at'].append((aa[rs, sl] * w_ex[:, sl]).astype(BF16))
                chain['rt'].append((ctx['r'][rs, sl] * w_in[:, sl]).astype(BF16))
                chain['bt'].append((bb[rs, sl] * w_inv[:, sl]).astype(BF16))
                chain['kt'].append((ctx['k2'][rs, sl] * w_inv[:, sl]).astype(BF16))
                chain['vb'].append(ctx['v'][rs, sl].astype(BF16))
                chain['w_last'].append(w_in[c - 1:c, sl])
        ctx.update(chain)
        ctx['s0'] = [s_scr[n, :, sl] for n in seqs for sl in groups]

    def store_state():
        for n in seqs:
            for gi, sl in enumerate(groups):
                s_scr[n, :, sl] = ctx['s_new'][n * N_GROUPS + gi]
        ctx['y'] = jnp.concatenate([jnp.concatenate(ctx['y'][n * N_GROUPS:(n + 1) * N_GROUPS], axis=1)
                                    for n in seqs], axis=0)

    def bonus():
        ctx['bonus'] = _bonus(ctx['r'], ctx['k2'], ctx['v'], rk_ref[...], bones)

    def store_out():
        rw_ref[...] = ctx['out'].reshape(nb, c, RWKV_W).astype(rw_ref.dtype)

    _run(_prelude_stages(ctx, *pre_params) + [cumulative_decay, scale]
         + _chunk_stages(ctx, *chain_consts) + [store_state, bonus]
         + _postlude_stages(ctx, lng_ref[...], lnb_ref[...], bones) + [store_out])

    @pl.when(t == pl.num_programs(1) - 1)
    def _():
        sfin_ref[...] = s_scr[...]


def _chunk_constants():
    c = CHUNK
    t = np.arange(c)[:, None]
    s = (np.arange(GROUP_HEADS * c) % c)[None, :]
    strict = (s < t).astype(np.float32)
    incl = (s <= t).astype(np.float32)
    iden = (s == t).astype(np.float32)
    lane_low = np.broadcast_to(np.arange(LANES)[None, :] < HEAD_DIM, (c, LANES))
    half = np.stack([lane_low, ~lane_low]).astype(np.float32)
    tri = (np.arange(c)[None, :] <= np.arange(c)[:, None]).astype(np.float32)
    tri = np.concatenate([tri, tri], axis=1)
    return (jnp.asarray(strict), jnp.asarray(incl), jnp.asarray(iden),
            jnp.asarray(half, dtype=BF16), jnp.asarray(tri, dtype=BF16))


def _head_ones():
    h = np.arange(GROUP_W) // HEAD_DIM
    return jnp.asarray((h[:, None] == h[None, :]).astype(np.float32), dtype=BF16)


def _rwkv_chunked(p, rows_prm, w2, gl_up, bones):
    b, t, _ = p.shape
    c = CHUNK
    nb = SEQ_BLOCK
    consts = _chunk_constants()
    const2 = lambda arr: pl.BlockSpec(arr.shape, lambda i, j: (0,) * arr.ndim)
    operands = list(rows_prm) + [w2, gl_up, bones] + list(consts)
    return pl.pallas_call(
        _rwkv_chunk_kernel,
        grid=(b // nb, t // c),
        in_specs=[pl.BlockSpec((nb, c, SHIFT_W), lambda i, j: (i, j, 0))] + [const2(a) for a in operands],
        out_specs=[
            pl.BlockSpec((nb, c, RWKV_W), lambda i, j: (i, j, 0)),
            pl.BlockSpec((nb, HEAD_DIM, RWKV_W), lambda i, j: (i, 0, 0)),
        ],
        out_shape=[
            jax.ShapeDtypeStruct((b, t, RWKV_W), BF16),
            jax.ShapeDtypeStruct((b, HEAD_DIM, RWKV_W), F32),
        ],
        scratch_shapes=[pltpu.VMEM((nb, HEAD_DIM, RWKV_W), F32)],
        compiler_params=pltpu.CompilerParams(
            dimension_semantics=("arbitrary", "arbitrary"), vmem_limit_bytes=VMEM_LIMIT),
        name="rwkv_chunked",
    )(p, *operands)


def _outproj_mlp_kernel(x_ref, po_ref, rw_ref, wout_ref, g2_ref, wup_ref, wdn_ref, gf_ref, y_ref):
    mix = _dot(po_ref[...], wout_ref[0:POOL_W, :]) + _dot(rw_ref[...], wout_ref[POOL_W:, :])
    x1 = x_ref[...] + mix
    h2 = _rmsnorm(x1, g2_ref[...]).astype(BF16)
    acc = jnp.zeros_like(x1)
    for j in range(D_FF // FF_BLOCK):
        cols = slice(j * FF_BLOCK, (j + 1) * FF_BLOCK)
        hid = jnp.maximum(_dot(h2, wup_ref[:, cols]), 0.0)
        acc = acc + _dot((hid * hid).astype(BF16), wdn_ref[cols, :])
    y_ref[...] = _rmsnorm(x1 + acc, gf_ref[...])


def _outproj_mlp(x, po, rw, wout, g2, wup, wdn, gf):
    n = x.shape[0]
    rows = min(MLP_ROW_BLOCK, n)
    const = lambda *shape: pl.BlockSpec(shape, lambda i: (0,) * len(shape), pipeline_mode=pl.Buffered(1))
    return pl.pallas_call(
        _outproj_mlp_kernel,
        grid=(n // rows,),
        in_specs=[
            pl.BlockSpec((rows, D_MODEL), lambda i: (i, 0)),
            pl.BlockSpec((rows, POOL_W), lambda i: (i, 0)),
            pl.BlockSpec((rows, RWKV_W), lambda i: (i, 0)),
            const(D_MODEL, D_MODEL),
            const(1, D_MODEL),
            const(D_MODEL, D_FF),
            const(D_FF, D_MODEL),
            const(1, D_MODEL),
        ],
        out_specs=pl.BlockSpec((rows, D_MODEL), lambda i: (i, 0)),
        out_shape=jax.ShapeDtypeStruct((n, D_MODEL), F32),
        compiler_params=pltpu.CompilerParams(
            dimension_semantics=("arbitrary",), vmem_limit_bytes=VMEM_LIMIT),
        name="outproj_mlp",
    )(x, po, rw, wout, g2, wup, wdn, gf)


def _sample_front_kernel(x_ref, sp_ref, sh_ref, g1_ref, win_ref, pw_ref, ps_ref, mu_ref, w0_ref, a0_ref,
                         kk_ref, ka_ref, w2_ref, glup_ref, bones_ref,
                         u_ref, p_ref, po_ref, tvec_ref, vec_ref):
    h = _rmsnorm(x_ref[...], g1_ref[...]).astype(BF16)
    proj = _dot(h, win_ref[...])
    u = proj[:, :POOL_W]
    pf = proj[:, POOL_W:]
    u_ref[...] = u
    p_ref[...] = pf

    outs = []
    for gi, win in enumerate(POOL_WINDOWS):
        cols = slice(gi * POOL_GC, (gi + 1) * POOL_GC)
        acc = u[:, cols]
        for j in range(1, win):
            acc = acc + sp_ref[POOL_BUF - j, :, cols]
        d = acc * (1.0 / win) - u[:, cols]
        outs.append(_dot(d.astype(BF16), pw_ref[gi]))
    po_ref[...] = (jnp.concatenate(outs, axis=1) * ps_ref[...]).astype(po_ref.dtype)

    bones = bones_ref[...]
    r, k2, v, logw, a, g, kk = _rwkv_prelude(
        _token_shift(pf, sh_ref[...], mu_ref[...]), w0_ref[...], a0_ref[...], kk_ref[...], ka_ref[...],
        w2_ref[...], glup_ref[...], bones)
    w = jnp.exp(logw)
    aa = -kk
    bb = kk * a
    q = w * r + aa * _head_sum(bb * r, bones)
    yv = v * _head_sum(k2 * r, bones)
    for i, x in enumerate((aa, q, w, bb, k2, v)):
        tvec_ref[i] = x.T
    for i, x in enumerate((yv, r, k2, v, g)):
        vec_ref[i] = x


def _sample_front(x, sp, sh, g1, win, pw, ps, rows_prm, w2, gl_up, bones):
    n = x.shape[0]
    mu, w0, a0, k_k, k_a = rows_prm
    return pl.pallas_call(
        _sample_front_kernel,
        out_shape=[
            jax.ShapeDtypeStruct((n, POOL_W), F32),
            jax.ShapeDtypeStruct((n, SHIFT_W), F32),
            jax.ShapeDtypeStruct((n, POOL_W), BF16),
            jax.ShapeDtypeStruct((6, RWKV_W, n), F32),
            jax.ShapeDtypeStruct((5, n, RWKV_W), F32),
        ],
        compiler_params=pltpu.CompilerParams(vmem_limit_bytes=VMEM_LIMIT),
        name="sample_front",
    )(x, sp, sh, g1, win, pw, ps, mu, w0, a0, k_k, k_a, w2, gl_up, bones)


def _decode_state_kernel(s_ref, tvec_ref, snew_ref, yt_ref):
    rows = s_ref.shape[0]
    v0 = pl.program_id(1) * rows
    aa = tvec_ref[0]
    q = tvec_ref[1]
    w = tvec_ref[2]
    bb = tvec_ref[3]
    k = tvec_ref[4]
    for i in range(rows):
        s0 = s_ref[i]
        v_row = tvec_ref[5, pl.ds(v0 + i, 1), :]
        sa = jnp.sum(s0 * aa, axis=0, keepdims=True)
        yt_ref[i:i + 1, :] = jnp.sum(s0 * q, axis=0, keepdims=True)
        snew_ref[i] = s0 * w + sa * bb + v_row * k


def _decode_state(s, tvec):
    n = s.shape[-1]
    rows = DEC_ROWS
    return pl.pallas_call(
        _decode_state_kernel,
        grid=(N_HEADS, HEAD_DIM // rows),
        in_specs=[
            pl.BlockSpec((None, rows, HEAD_DIM, n), lambda h, j: (h, j, 0, 0)),
            pl.BlockSpec((6, None, HEAD_DIM, n), lambda h, j: (0, h, 0, 0)),
        ],
        out_specs=[
            pl.BlockSpec((None, rows, HEAD_DIM, n), lambda h, j: (h, j, 0, 0)),
            pl.BlockSpec((None, rows, n), lambda h, j: (h, j, 0)),
        ],
        out_shape=[
            jax.ShapeDtypeStruct(s.shape, F32),
            jax.ShapeDtypeStruct((N_HEADS, HEAD_DIM, n), F32),
        ],
        compiler_params=pltpu.CompilerParams(
            dimension_semantics=("arbitrary", "arbitrary"), vmem_limit_bytes=VMEM_LIMIT),
        name="decode_state",
    )(s, tvec)


def _sample_post_kernel(y0t_ref, vec_ref, rk_ref, lng_ref, lnb_ref, bones_ref, rw_ref):
    y = y0t_ref[...].T + vec_ref[0]
    out = _rwkv_postlude(y, vec_ref[1], vec_ref[2], vec_ref[3], vec_ref[4],
                         rk_ref[...], lng_ref[...], lnb_ref[...], bones_ref[...])
    rw_ref[...] = out.astype(rw_ref.dtype)


def _sample_post(y0t, vec, r_k, ln_g, ln_b, bones):
    n = y0t.shape[1]
    return pl.pallas_call(
        _sample_post_kernel,
        out_shape=jax.ShapeDtypeStruct((n, RWKV_W), BF16),
        compiler_params=pltpu.CompilerParams(vmem_limit_bytes=VMEM_LIMIT),
        name="sample_post",
    )(y0t, vec, r_k, ln_g, ln_b, bones)


def kernel(x_prompt, x_sample, state_wkv, state_shift, state_pool, norm1_g, w_in, shift_mu, pool_w, pool_scale,
           w0, w_lora_up, a0, a_lora_up, g_lora_up, k_k, k_a, r_k, ln_x_g, ln_x_b, w_out, norm2_g, w_up, w_down,
           norm_f_g):
    depth = w_in.shape[0]
    assert depth == 1, "single-layer step"
    b, t, _ = x_prompt.shape
    n = x_sample.shape[0]
    row = lambda a: a.reshape(1, -1).astype(F32)

    g1 = row(norm1_g[0])
    win = w_in[0].astype(BF16)
    pw = pool_w[0].astype(BF16)
    ps = row(pool_scale[0])
    zeros = jnp.zeros((64, RWKV_W), F32)
    w2 = jnp.concatenate([jnp.concatenate([w_lora_up[0], zeros], axis=1),
                          jnp.concatenate([zeros, a_lora_up[0]], axis=1)], axis=0).astype(BF16)
    gl_up = g_lora_up[0].astype(BF16)
    bones = _head_ones()
    mu = row(shift_mu[0])
    pre_rows = (row(w0[0]), row(a0[0]), row(k_k[0]), row(k_a[0]))
    post_rows = (row(r_k[0]), row(ln_x_g[0]), row(ln_x_b[0]))
    wout = w_out[0].astype(BF16)
    g2 = row(norm2_g[0])
    wup = w_up[0].astype(BF16)
    wdn = w_down[0].astype(BF16)
    gf = row(norm_f_g)

    sh_p, po_p, plast, ulast = _inproj_pool(x_prompt, g1, win, pw, ps, mu)
    rw_p, s_nat = _rwkv_chunked(sh_p, pre_rows + post_rows, w2, gl_up, bones)
    y_p = _outproj_mlp(x_prompt.reshape(b * t, D_MODEL), po_p.reshape(b * t, POOL_W),
                       rw_p.reshape(b * t, RWKV_W), wout, g2, wup, wdn, gf).reshape(b, t, D_MODEL)
    wkv_p = jnp.transpose(s_nat.reshape(b, HEAD_DIM, N_HEADS, HEAD_DIM), (0, 2, 1, 3))[None]
    shift_p = plast[:, 0:1, :][None]
    pool_p = ulast[:, POOL_HALO - POOL_BUF:, :][None]

    xs = x_sample.reshape(n, D_MODEL)
    sp = state_pool[0]
    u_s, p_s, po_s, tvec, vec = _sample_front(xs, jnp.transpose(sp, (1, 0, 2)), state_shift[0].reshape(n, SHIFT_W),
                                              g1, win, pw, ps, (mu,) + pre_rows, w2, gl_up, bones)
    s_t = jnp.transpose(state_wkv[0], (1, 2, 3, 0))
    s_new_t, y0t = _decode_state(s_t, tvec.reshape(6, N_HEADS, HEAD_DIM, n))
    s_new = jnp.transpose(s_new_t, (3, 0, 1, 2))
    rw_s = _sample_post(y0t.reshape(RWKV_W, n), vec, *post_rows, bones)
    y_s = _outproj_mlp(xs, po_s, rw_s, wout, g2, wup, wdn, gf).reshape(n, 1, D_MODEL)
    shift_s = p_s.reshape(1, n, 1, SHIFT_W)
    pool_s = jnp.concatenate([sp[:, 1:, :], u_s[:, None, :]], axis=1)[None]

    return (y_p, y_s, wkv_p, shift_p, pool_p, s_new[None], shift_s, pool_s)
```

```python
import jax
import jax.numpy as jnp
import numpy as np
from jax import lax
from jax.experimental import pallas as pl
from jax.experimental.pallas import tpu as pltpu

F32 = jnp.float32
BF16 = jnp.bfloat16

D_MODEL = 1024
POOL_W = 512
RWKV_W = 512
POOL_WINDOWS = (2, 4, 8, 16)
POOL_GC = 128
POOL_BUF = 15
HEAD_DIM = 64
N_HEADS = 8
SHIFT_W = 1792
IN_W = POOL_W + SHIFT_W
D_FF = 4096
LORA_Z0 = 3 * RWKV_W
GATE_Z0 = LORA_Z0 + 128
RMS_EPS = 1e-6
GN_EPS = 64e-5
NORM_EPS = 1e-12
DECAY_SCALE = float(np.exp(-0.5))

LANES = 128
CHUNK = 64
SEQ_BLOCK = 8
GROUP_HEADS = 4
GROUP_W = GROUP_HEADS * HEAD_DIM
N_GROUPS = N_HEADS // GROUP_HEADS
POOL_HALO = 16
ROW_BLOCK = 1024
MLP_ROW_BLOCK = 1024
FF_BLOCK = 1024
DEC_ROWS = 32
VMEM_LIMIT = 56 * 1024 * 1024


def _dot(a, b):
    return jnp.dot(a, b, preferred_element_type=F32)


def _dot_nt(a, b):
    return lax.dot_general(a, b, (((1,), (1,)), ((), ())), preferred_element_type=F32)


def _rmsnorm(x, g):
    return x * lax.rsqrt(jnp.mean(x * x, axis=-1, keepdims=True) + RMS_EPS) * g


def _sigmoid(x):
    return 1.0 / (1.0 + jnp.exp(-x))


def _head_sum(x, bones):
    parts = [_dot(x[:, g * GROUP_W:(g + 1) * GROUP_W].astype(BF16), bones) for g in range(N_GROUPS)]
    return jnp.concatenate(parts, axis=1)


def _token_shift(pf, prev, mu):
    return pf + (prev - pf) * mu


def _prelude_stages(ctx, w0, a0, k_k, k_a, w2, gl_up, bones):
    def split():
        ps = ctx['ps']
        ctx['r'] = ps[:, 0:RWKV_W]
        ctx['k'] = ps[:, RWKV_W:2 * RWKV_W]
        ctx['v'] = ps[:, 2 * RWKV_W:3 * RWKV_W]
        z = ps[:, LORA_Z0:GATE_Z0]
        lane = lax.broadcasted_iota(jnp.int32, z.shape, 1)
        ctx['zt'] = jnp.where(lane < 64, jnp.tanh(z), z).astype(BF16)
        ctx['zg'] = _sigmoid(ps[:, GATE_Z0:SHIFT_W]).astype(BF16)

    def lora():
        ctx['wa'] = _dot(ctx['zt'], w2)

    def decay():
        wa = ctx['wa']
        ctx['logw'] = -DECAY_SCALE * _sigmoid(w0 + wa[:, :RWKV_W])
        ctx['a'] = _sigmoid(a0 + wa[:, RWKV_W:])

    def gate():
        ctx['g'] = _dot(ctx['zg'], gl_up)

    def key_norm():
        kk = ctx['k'] * k_k
        ctx['kk'] = kk
        ctx['kk_ss'] = _head_sum(kk * kk, bones)

    def keys():
        ctx['kk'] = ctx['kk'] * lax.rsqrt(ctx['kk_ss'] + NORM_EPS)
        ctx['k2'] = ctx['k'] * (1.0 + (ctx['a'] - 1.0) * k_a)

    return [split, lora, decay, gate, key_norm, keys]


def _bonus(r, k2, v, r_k, bones):
    return _head_sum(r * k2 * r_k, bones) * v


def _postlude_stages(ctx, ln_g, ln_b, bones):
    def mean():
        ctx['yc'] = ctx['y'] - _head_sum(ctx['y'], bones) * (1.0 / HEAD_DIM)

    def var():
        ctx['var'] = _head_sum(ctx['yc'] * ctx['yc'], bones) * (1.0 / HEAD_DIM)

    def gate():
        yn = ctx['yc'] * lax.rsqrt(ctx['var'] + GN_EPS) * ln_g + ln_b
        ctx['out'] = (yn + ctx['bonus']) * ctx['g']

    return [mean, var, gate]


def _run(stages):
    for stage in stages:
        stage()


def _rwkv_prelude(ps, *params):
    ctx = {'ps': ps}
    _run(_prelude_stages(ctx, *params))
    return tuple(ctx[name] for name in ('r', 'k2', 'v', 'logw', 'a', 'g', 'kk'))


def _rwkv_postlude(y, r, k2, v, g, r_k, ln_g, ln_b, bones):
    ctx = {'y': y, 'bonus': _bonus(r, k2, v, r_k, bones), 'g': g}
    _run(_postlude_stages(ctx, ln_g, ln_b, bones))
    return ctx['out']


def _inproj_pool_kernel(x_ref, g1_ref, win_ref, pw_ref, ps_ref, mu_ref, sh_ref, po_ref, plast_ref, ulast_ref,
                        ext_ref, prow_ref):
    tb = pl.program_id(1)
    rows = x_ref.shape[0]

    @pl.when(tb == 0)
    def _():
        ext_ref[0:POOL_HALO, :] = jnp.zeros((POOL_HALO, POOL_W), F32)
        prow_ref[...] = jnp.zeros_like(prow_ref)

    h = _rmsnorm(x_ref[...], g1_ref[...]).astype(BF16)
    proj = _dot(h, win_ref[...])
    u = proj[:, :POOL_W]
    ext_ref[POOL_HALO:POOL_HALO + rows, :] = u
    ulast_ref[...] = u[rows - POOL_HALO:, :]

    pf = proj[:, POOL_W:]
    first = lax.broadcasted_iota(jnp.int32, (rows, 1), 0) == 0
    prev = jnp.where(first, prow_ref[0:1, :], pltpu.roll(pf, 1, 0))
    sh_ref[...] = _token_shift(pf, prev, mu_ref[...])
    prow_ref[0:1, :] = pf[rows - 1:rows, :]
    plast_ref[...] = jnp.broadcast_to(pf[rows - 1:rows, :], plast_ref.shape)

    pos = tb * rows + lax.broadcasted_iota(jnp.int32, (rows, 1), 0)
    outs = []
    for gi, win in enumerate(POOL_WINDOWS):
        cols = slice(gi * POOL_GC, (gi + 1) * POOL_GC)
        acc = ext_ref[:, cols]
        shift = 1
        while shift < win:
            acc = acc + pltpu.roll(acc, shift, 0)
            shift *= 2
        acc = acc[POOL_HALO:, :]
        cnt = jnp.minimum(win, pos + 1).astype(F32)
        d = acc / cnt - u[:, cols]
        outs.append(_dot(d.astype(BF16), pw_ref[gi]))
    po_ref[...] = (jnp.concatenate(outs, axis=1) * ps_ref[...]).astype(po_ref.dtype)
    ext_ref[0:POOL_HALO, :] = ext_ref[rows:rows + POOL_HALO, :]


def _inproj_pool(x, g1, win, pw, ps, mu):
    b, t, _ = x.shape
    rows = ROW_BLOCK
    const = lambda *shape: pl.BlockSpec(shape, lambda i, j: (0,) * len(shape), pipeline_mode=pl.Buffered(1))
    return pl.pallas_call(
        _inproj_pool_kernel,
        grid=(b, t // rows),
        in_specs=[
            pl.BlockSpec((None, rows, D_MODEL), lambda i, j: (i, j, 0)),
            const(1, D_MODEL),
            const(D_MODEL, IN_W),
            const(len(POOL_WINDOWS), POOL_GC, POOL_GC),
            const(1, POOL_W),
            const(1, SHIFT_W),
        ],
        out_specs=[
            pl.BlockSpec((None, rows, SHIFT_W), lambda i, j: (i, j, 0)),
            pl.BlockSpec((None, rows, POOL_W), lambda i, j: (i, j, 0)),
            pl.BlockSpec((None, 8, SHIFT_W), lambda i, j: (i, 0, 0)),
            pl.BlockSpec((None, POOL_HALO, POOL_W), lambda i, j: (i, 0, 0)),
        ],
        out_shape=[
            jax.ShapeDtypeStruct((b, t, SHIFT_W), F32),
            jax.ShapeDtypeStruct((b, t, POOL_W), BF16),
            jax.ShapeDtypeStruct((b, 8, SHIFT_W), F32),
            jax.ShapeDtypeStruct((b, POOL_HALO, POOL_W), F32),
        ],
        scratch_shapes=[pltpu.VMEM((rows + POOL_HALO, POOL_W), F32), pltpu.VMEM((8, SHIFT_W), F32)],
        compiler_params=pltpu.CompilerParams(
            dimension_semantics=("arbitrary", "arbitrary"), vmem_limit_bytes=VMEM_LIMIT),
        name="inproj_pool",
    )(x, g1, win, pw, ps, mu)


def _expand(x_bf, half):
    zero = jnp.zeros((x_bf.shape[0], LANES), x_bf.dtype)
    blocks = []
    for h in range(GROUP_HEADS):
        tile = h // 2
        piece = x_bf[:, tile * LANES:(tile + 1) * LANES] * half[h % 2]
        blocks.append(jnp.concatenate([piece, zero] if tile == 0 else [zero, piece], axis=1))
    return jnp.concatenate(blocks, axis=0)


def _diag_blocks(full, low):
    tiles = []
    for tile in range(GROUP_HEADS // 2):
        cols = slice(tile * LANES, (tile + 1) * LANES)
        left = full[(2 * tile) * HEAD_DIM:(2 * tile + 1) * HEAD_DIM, cols]
        right = full[(2 * tile + 1) * HEAD_DIM:(2 * tile + 2) * HEAD_DIM, cols]
        tiles.append(jnp.where(low, left, right))
    return jnp.concatenate(tiles, axis=1)


def _chunk_stages(ctx, strict, incl, iden, half):
    c = CHUNK
    each = lambda fn, *names: [fn(*args) for args in zip(*(ctx[name] for name in names))]
    bf = lambda x: x.astype(BF16)
    blockdiag = lambda x: _expand(bf(x), half)
    rows2 = lambda x, y: jnp.concatenate([x, y], axis=0)

    def scores():
        ctx['ar'] = each(rows2, 'at', 'rt')
        e_bk = each(lambda b_, k_: rows2(_expand(b_, half), _expand(k_, half)), 'bt', 'kt')
        ctx['sc'] = [_dot_nt(a_, e_) for a_, e_ in zip(ctx['ar'], e_bk)]

    def masks():
        ctx['lm_k'] = each(lambda x: bf(rows2(x[:c, GROUP_W:] * strict, x[c:, GROUP_W:] * incl)), 'sc')
        ctx['m_rb'] = each(lambda x: bf(x[c:, :GROUP_W] * incl), 'sc')
        ctx['p'] = each(lambda x: x[:c, :GROUP_W] * strict, 'sc')
        ctx['n'] = each(lambda x: iden + x, 'p')
        del ctx['sc']

    def square():
        ctx['p'] = each(lambda x: _dot(bf(x), blockdiag(x)), 'p')

    def double():
        out = each(lambda p_, n_: _dot(bf(rows2(p_, n_)), blockdiag(p_)), 'p', 'n')
        ctx['p'] = [x[:c] for x in out]
        ctx['n'] = [n_ + x[c:] for n_, x in zip(ctx['n'], out)]

    def inverse():
        ctx['t_inv'] = each(lambda n_, p_: bf(n_ + _dot(bf(n_), blockdiag(p_))), 'n', 'p')

    def values():
        ctx['lv'] = each(lambda m_, v_: _dot(m_, _expand(v_, half)), 'lm_k', 'vb')

    def state_in():
        ctx['ps'] = each(lambda a_, s_: _dot_nt(a_, blockdiag(s_)), 'ar', 's0')

    def solve():
        ctx['ub'] = each(lambda t_, p_, l_: bf(_dot(t_, blockdiag(p_[:c] + l_[:c]))), 't_inv', 'ps', 'lv')

    def outputs():
        ctx['y'] = each(lambda p_, m_, u_, l_: p_[c:] + _dot(m_, _expand(u_, half)) + l_[c:], 'ps', 'm_rb', 'ub', 'lv')

    def state_out():
        low = lax.broadcasted_iota(jnp.int32, (HEAD_DIM, LANES), 1) < HEAD_DIM
        full = each(lambda u_, v_, b_, k_: lax.dot_general(rows2(u_, v_), rows2(b_, k_), (((0,), (0,)), ((), ())),
                                                           preferred_element_type=F32), 'ub', 'vb', 'bt', 'kt')
        ctx['s_new'] = [(s_ + _diag_blocks(f_, low)) * w_ for s_, f_, w_ in zip(ctx['s0'], full, ctx['w_last'])]

    levels = int(np.log2(c))
    return ([scores, masks, square] + [double] * (levels - 2)
            + [inverse, values, state_in, solve, outputs, state_out])


def _rwkv_chunk_kernel(sh_ref, w0_ref, a0_ref, kk_ref, ka_ref, rk_ref, lng_ref, lnb_ref,
                       w2_ref, glup_ref, bones_ref, strict_ref, incl_ref, iden_ref, half_ref, tri_ref,
                       rw_ref, sfin_ref, s_scr):
    t = pl.program_id(1)
    nb, c, _ = sh_ref.shape
    seqs = range(nb)

    @pl.when(t == 0)
    def _():
        s_scr[...] = jnp.zeros_like(s_scr)

    bones = bones_ref[...]
    pre_params = (w0_ref[...], a0_ref[...], kk_ref[...], ka_ref[...], w2_ref[...], glup_ref[...], bones)
    chain_consts = (strict_ref[...], incl_ref[...], iden_ref[...], half_ref[...])
    tri = tri_ref[...]
    groups = [slice(gi * GROUP_W, (gi + 1) * GROUP_W) for gi in range(N_GROUPS)]
    rows = [slice(n * c, (n + 1) * c) for n in seqs]
    ctx = {'ps': jnp.concatenate([sh_ref[n] for n in seqs], axis=0)}

    def cumulative_decay():
        logw = ctx['logw']
        hi = logw.astype(BF16)
        lo = (logw - hi.astype(F32)).astype(BF16)
        ctx['cl'] = [_dot(tri, jnp.concatenate([hi[rs], lo[rs]], axis=0)) for rs in rows]

    def scale():
        aa = -ctx['kk']
        bb = ctx['kk'] * ctx['a']
        chain = {name: [] for name in ('at', 'rt', 'bt', 'kt', 'vb', 'w_last')}
        for rs, cl in zip(rows, ctx['cl']):
            w_in = jnp.exp(cl)
            w_ex = jnp.exp(cl - ctx['logw'][rs])
            w_inv = jnp.exp(-cl)
            for sl in groups:
                chain['at'].append((aa[rs, sl] * w_ex[:, sl]).astype(BF16))
                chain['rt'].append((ctx['r'][rs, sl] * w_in[:, sl]).astype(BF16))
                chain['bt'].append((bb[rs, sl] * w_inv[:, sl]).astype(BF16))
                chain['kt'].append((ctx['k2'][rs, sl] * w_inv[:, sl]).astype(BF16))
                chain['vb'].append(ctx['v'][rs, sl].astype(BF16))
                chain['w_last'].append(w_in[c - 1:c, sl])
        ctx.update(chain)
        ctx['s0'] = [s_scr[n, :, sl] for n in seqs for sl in groups]

    def store_state():
        for n in seqs:
            for gi, sl in enumerate(groups):
                s_scr[n, :, sl] = ctx['s_new'][n * N_GROUPS + gi]
        ctx['y'] = jnp.concatenate([jnp.concatenate(ctx['y'][n * N_GROUPS:(n + 1) * N_GROUPS], axis=1)
                                    for n in seqs], axis=0)

    def bonus():
        ctx['bonus'] = _bonus(ctx['r'], ctx['k2'], ctx['v'], rk_ref[...], bones)

    def store_out():
        rw_ref[...] = ctx['out'].reshape(nb, c, RWKV_W).astype(rw_ref.dtype)

    _run(_prelude_stages(ctx, *pre_params) + [cumulative_decay, scale]
         + _chunk_stages(ctx, *chain_consts) + [store_state, bonus]
         + _postlude_stages(ctx, lng_ref[...], lnb_ref[...], bones) + [store_out])

    @pl.when(t == pl.num_programs(1) - 1)
    def _():
        sfin_ref[...] = s_scr[...]


def _chunk_constants():
    c = CHUNK
    t = np.arange(c)[:, None]
    s = (np.arange(GROUP_HEADS * c) % c)[None, :]
    strict = (s < t).astype(np.float32)
    incl = (s <= t).astype(np.float32)
    iden = (s == t).astype(np.float32)
    lane_low = np.broadcast_to(np.arange(LANES)[None, :] < HEAD_DIM, (c, LANES))
    half = np.stack([lane_low, ~lane_low]).astype(np.float32)
    tri = (np.arange(c)[None, :] <= np.arange(c)[:, None]).astype(np.float32)
    tri = np.concatenate([tri, tri], axis=1)
    return (jnp.asarray(strict), jnp.asarray(incl), jnp.asarray(iden),
            jnp.asarray(half, dtype=BF16), jnp.asarray(tri, dtype=BF16))


def _head_ones():
    h = np.arange(GROUP_W) // HEAD_DIM
    return jnp.asarray((h[:, None] == h[None, :]).astype(np.float32), dtype=BF16)


def _rwkv_chunked(p, rows_prm, w2, gl_up, bones):
    b, t, _ = p.shape
    c = CHUNK
    nb = SEQ_BLOCK
    consts = _chunk_constants()
    const2 = lambda arr: pl.BlockSpec(arr.shape, lambda i, j: (0,) * arr.ndim)
    operands = list(rows_prm) + [w2, gl_up, bones] + list(consts)
    return pl.pallas_call(
        _rwkv_chunk_kernel,
        grid=(b // nb, t // c),
        in_specs=[pl.BlockSpec((nb, c, SHIFT_W), lambda i, j: (i, j, 0))] + [const2(a) for a in operands],
        out_specs=[
            pl.BlockSpec((nb, c, RWKV_W), lambda i, j: (i, j, 0)),
            pl.BlockSpec((nb, HEAD_DIM, RWKV_W), lambda i, j: (i, 0, 0)),
        ],
        out_shape=[
            jax.ShapeDtypeStruct((b, t, RWKV_W), BF16),
            jax.ShapeDtypeStruct((b, HEAD_DIM, RWKV_W), F32),
        ],
        scratch_shapes=[pltpu.VMEM((nb, HEAD_DIM, RWKV_W), F32)],
        compiler_params=pltpu.CompilerParams(
            dimension_semantics=("arbitrary", "arbitrary"), vmem_limit_bytes=VMEM_LIMIT),
        name="rwkv_chunked",
    )(p, *operands)


def _outproj_mlp_kernel(x_ref, po_ref, rw_ref, wout_ref, g2_ref, wup_ref, wdn_ref, gf_ref, y_ref):
    mix = _dot(po_ref[...], wout_ref[0:POOL_W, :]) + _dot(rw_ref[...], wout_ref[POOL_W:, :])
    x1 = x_ref[...] + mix
    h2 = _rmsnorm(x1, g2_ref[...]).astype(BF16)
    acc = jnp.zeros_like(x1)
    for j in range(D_FF // FF_BLOCK):
        cols = slice(j * FF_BLOCK, (j + 1) * FF_BLOCK)
        hid = jnp.maximum(_dot(h2, wup_ref[:, cols]), 0.0)
        acc = acc + _dot((hid * hid).astype(BF16), wdn_ref[cols, :])
    y_ref[...] = _rmsnorm(x1 + acc, gf_ref[...])


def _outproj_mlp(x, po, rw, wout, g2, wup, wdn, gf):
    n = x.shape[0]
    rows = min(MLP_ROW_BLOCK, n)
    const = lambda *shape: pl.BlockSpec(shape, lambda i: (0,) * len(shape), pipeline_mode=pl.Buffered(1))
    return pl.pallas_call(
        _outproj_mlp_kernel,
        grid=(n // rows,),
        in_specs=[
            pl.BlockSpec((rows, D_MODEL), lambda i: (i, 0)),
            pl.BlockSpec((rows, POOL_W), lambda i: (i, 0)),
            pl.BlockSpec((rows, RWKV_W), lambda i: (i, 0)),
            const(D_MODEL, D_MODEL),
            const(1, D_MODEL),
            const(D_MODEL, D_FF),
            const(D_FF, D_MODEL),
            const(1, D_MODEL),
        ],
        out_specs=pl.BlockSpec((rows, D_MODEL), lambda i: (i, 0)),
        out_shape=jax.ShapeDtypeStruct((n, D_MODEL), F32),
        compiler_params=pltpu.CompilerParams(
            dimension_semantics=("arbitrary",), vmem_limit_bytes=VMEM_LIMIT),
        name="outproj_mlp",
    )(x, po, rw, wout, g2, wup, wdn, gf)


def _sample_front_kernel(x_ref, sp_ref, sh_ref, g1_ref, win_ref, pw_ref, ps_ref, mu_ref, w0_ref, a0_ref,
                         kk_ref, ka_ref, w2_ref, glup_ref, bones_ref,
                         u_ref, p_ref, po_ref, tvec_ref, vec_ref):
    h = _rmsnorm(x_ref[...], g1_ref[...]).astype(BF16)
    proj = _dot(h, win_ref[...])
    u = proj[:, :POOL_W]
    pf = proj[:, POOL_W:]
    u_ref[...] = u
    p_ref[...] = pf

    outs = []
    for gi, win in enumerate(POOL_WINDOWS):
        cols = slice(gi * POOL_GC, (gi + 1) * POOL_GC)
        acc = u[:, cols]
        for j in range(1, win):
            acc = acc + sp_ref[POOL_BUF - j, :, cols]
        d = acc * (1.0 / win) - u[:, cols]
        outs.append(_dot(d.astype(BF16), pw_ref[gi]))
    po_ref[...] = (jnp.concatenate(outs, axis=1) * ps_ref[...]).astype(po_ref.dtype)

    bones = bones_ref[...]
    r, k2, v, logw, a, g, kk = _rwkv_prelude(
        _token_shift(pf, sh_ref[...], mu_ref[...]), w0_ref[...], a0_ref[...], kk_ref[...], ka_ref[...],
        w2_ref[...], glup_ref[...], bones)
    w = jnp.exp(logw)
    aa = -kk
    bb = kk * a
    q = w * r + aa * _head_sum(bb * r, bones)
    yv = v * _head_sum(k2 * r, bones)
    for i, x in enumerate((aa, q, w, bb, k2, v)):
        tvec_ref[i] = x.T
    for i, x in enumerate((yv, r, k2, v, g)):
        vec_ref[i] = x


def _sample_front(x, sp, sh, g1, win, pw, ps, rows_prm, w2, gl_up, bones):
    n = x.shape[0]
    mu, w0, a0, k_k, k_a = rows_prm
    return pl.pallas_call(
        _sample_front_kernel,
        out_shape=[
            jax.ShapeDtypeStruct((n, POOL_W), F32),
            jax.ShapeDtypeStruct((n, SHIFT_W), F32),
            jax.ShapeDtypeStruct((n, POOL_W), BF16),
            jax.ShapeDtypeStruct((6, RWKV_W, n), F32),
            jax.ShapeDtypeStruct((5, n, RWKV_W), F32),
        ],
        compiler_params=pltpu.CompilerParams(vmem_limit_bytes=VMEM_LIMIT),
        name="sample_front",
    )(x, sp, sh, g1, win, pw, ps, mu, w0, a0, k_k, k_a, w2, gl_up, bones)


def _decode_state_kernel(s_ref, tvec_ref, snew_ref, yt_ref):
    rows = s_ref.shape[0]
    v0 = pl.program_id(1) * rows
    aa = tvec_ref[0]
    q = tvec_ref[1]
    w = tvec_ref[2]
    bb = tvec_ref[3]
    k = tvec_ref[4]
    for i in range(rows):
        s0 = s_ref[i]
        v_row = tvec_ref[5, pl.ds(v0 + i, 1), :]
        sa = jnp.sum(s0 * aa, axis=0, keepdims=True)
        yt_ref[i:i + 1, :] = jnp.sum(s0 * q, axis=0, keepdims=True)
        snew_ref[i] = s0 * w + sa * bb + v_row * k


def _decode_state(s, tvec):
    n = s.shape[-1]
    rows = DEC_ROWS
    return pl.pallas_call(
        _decode_state_kernel,
        grid=(N_HEADS, HEAD_DIM // rows),
        in_specs=[
            pl.BlockSpec((None, rows, HEAD_DIM, n), lambda h, j: (h, j, 0, 0)),
            pl.BlockSpec((6, None, HEAD_DIM, n), lambda h, j: (0, h, 0, 0)),
        ],
        out_specs=[
            pl.BlockSpec((None, rows, HEAD_DIM, n), lambda h, j: (h, j, 0, 0)),
            pl.BlockSpec((None, rows, n), lambda h, j: (h, j, 0)),
        ],
        out_shape=[
            jax.ShapeDtypeStruct(s.shape, F32),
            jax.ShapeDtypeStruct((N_HEADS, HEAD_DIM, n), F32),
        ],
        compiler_params=pltpu.CompilerParams(
            dimension_semantics=("arbitrary", "arbitrary"), vmem_limit_bytes=VMEM_LIMIT),
        name="decode_state",
    )(s, tvec)


def _sample_post_kernel(y0t_ref, vec_ref, rk_ref, lng_ref, lnb_ref, bones_ref, rw_ref):
    y = y0t_ref[...].T + vec_ref[0]
    out = _rwkv_postlude(y, vec_ref[1], vec_ref[2], vec_ref[3], vec_ref[4],
                         rk_ref[...], lng_ref[...], lnb_ref[...], bones_ref[...])
    rw_ref[...] = out.astype(rw_ref.dtype)


def _sample_post(y0t, vec, r_k, ln_g, ln_b, bones):
    n = y0t.shape[1]
    return pl.pallas_call(
        _sample_post_kernel,
        out_shape=jax.ShapeDtypeStruct((n, RWKV_W), BF16),
        compiler_params=pltpu.CompilerParams(vmem_limit_bytes=VMEM_LIMIT),
        name="sample_post",
    )(y0t, vec, r_k, ln_g, ln_b, bones)


def kernel(x_prompt, x_sample, state_wkv, state_shift, state_pool, norm1_g, w_in, shift_mu, pool_w, pool_scale,
           w0, w_lora_up, a0, a_lora_up, g_lora_up, k_k, k_a, r_k, ln_x_g, ln_x_b, w_out, norm2_g, w_up, w_down,
           norm_f_g):
    depth = w_in.shape[0]
    assert depth == 1, "single-layer step"
    b, t, _ = x_prompt.shape
    n = x_sample.shape[0]
    row = lambda a: a.reshape(1, -1).astype(F32)

    g1 = row(norm1_g[0])
    win = w_in[0].astype(BF16)
    pw = pool_w[0].astype(BF16)
    ps = row(pool_scale[0])
    zeros = jnp.zeros((64, RWKV_W), F32)
    w2 = jnp.concatenate([jnp.concatenate([w_lora_up[0], zeros], axis=1),
                          jnp.concatenate([zeros, a_lora_up[0]], axis=1)], axis=0).astype(BF16)
    gl_up = g_lora_up[0].astype(BF16)
    bones = _head_ones()
    mu = row(shift_mu[0])
    pre_rows = (row(w0[0]), row(a0[0]), row(k_k[0]), row(k_a[0]))
    post_rows = (row(r_k[0]), row(ln_x_g[0]), row(ln_x_b[0]))
    wout = w_out[0].astype(BF16)
    g2 = row(norm2_g[0])
    wup = w_up[0].astype(BF16)
    wdn = w_down[0].astype(BF16)
    gf = row(norm_f_g)

    sh_p, po_p, plast, ulast = _inproj_pool(x_prompt, g1, win, pw, ps, mu)
    rw_p, s_nat = _rwkv_chunked(sh_p, pre_rows + post_rows, w2, gl_up, bones)
    y_p = _outproj_mlp(x_prompt.reshape(b * t, D_MODEL), po_p.reshape(b * t, POOL_W),
                       rw_p.reshape(b * t, RWKV_W), wout, g2, wup, wdn, gf).reshape(b, t, D_MODEL)
    wkv_p = jnp.transpose(s_nat.reshape(b, HEAD_DIM, N_HEADS, HEAD_DIM), (0, 2, 1, 3))[None]
    shift_p = plast[:, 0:1, :][None]
    pool_p = ulast[:, POOL_HALO - POOL_BUF:, :][None]

    xs = x_sample.reshape(n, D_MODEL)
    sp = state_pool[0]
    u_s, p_s, po_s, tvec, vec = _sample_front(xs, jnp.transpose(sp, (1, 0, 2)), state_shift[0].reshape(n, SHIFT_W),
                                              g1, win, pw, ps, (mu,) + pre_rows, w2, gl_up, bones)
    s_t = jnp.transpose(state_wkv[0], (1, 2, 3, 0))
    s_new_t, y0t = _decode_state(s_t, tvec.reshape(6, N_HEADS, HEAD_DIM, n))
    s_new = jnp.transpose(s_new_t, (3, 0, 1, 2))
    rw_s = _sample_post(y0t.reshape(RWKV_W, n), vec, *post_rows, bones)
    y_s = _outproj_mlp(xs, po_s, rw_s, wout, g2, wup, wdn, gf).reshape(n, 1, D_MODEL)
    shift_s = p_s.reshape(1, n, 1, SHIFT_W)
    pool_s = jnp.concatenate([sp[:, 1:, :], u_s[:, None, :]], axis=1)[None]

    return (y_p, y_s, wkv_p, shift_p, pool_p, s_new[None], shift_s, pool_s)
```

```python
import jax
import jax.numpy as jnp
import numpy as np
from jax import lax
from jax.experimental import pallas as pl
from jax.experimental.pallas import tpu as pltpu

F32 = jnp.float32
BF16 = jnp.bfloat16

D_MODEL = 1024
POOL_W = 512
RWKV_W = 512
POOL_WINDOWS = (2, 4, 8, 16)
POOL_GC = 128
POOL_BUF = 15
HEAD_DIM = 64
N_HEADS = 8
SHIFT_W = 1792
IN_W = POOL_W + SHIFT_W
D_FF = 4096
LORA_Z0 = 3 * RWKV_W
GATE_Z0 = LORA_Z0 + 128
RMS_EPS = 1e-6
GN_EPS = 64e-5
NORM_EPS = 1e-12
DECAY_SCALE = float(np.exp(-0.5))

LANES = 128
CHUNK = 64
SEQ_BLOCK = 8
GROUP_HEADS = 4
GROUP_W = GROUP_HEADS * HEAD_DIM
N_GROUPS = N_HEADS // GROUP_HEADS
POOL_HALO = 16
ROW_BLOCK = 1024
MLP_ROW_BLOCK = 1024
FF_BLOCK = 1024
DEC_ROWS = 64
VMEM_LIMIT = 56 * 1024 * 1024


def _dot(a, b):
    return jnp.dot(a, b, preferred_element_type=F32)


def _dot_nt(a, b):
    return lax.dot_general(a, b, (((1,), (1,)), ((), ())), preferred_element_type=F32)


def _rmsnorm(x, g):
    return x * lax.rsqrt(jnp.mean(x * x, axis=-1, keepdims=True) + RMS_EPS) * g


def _sigmoid(x):
    return 1.0 / (1.0 + jnp.exp(-x))


def _head_sum(x, bones):
    parts = [_dot(x[:, g * GROUP_W:(g + 1) * GROUP_W].astype(BF16), bones) for g in range(N_GROUPS)]
    return jnp.concatenate(parts, axis=1)


def _token_shift(pf, prev, mu):
    return pf + (prev - pf) * mu


def _prelude_stages(ctx, w0, a0, k_k, k_a, w2, gl_up, bones):
    def split():
        ps = ctx['ps']
        ctx['r'] = ps[:, 0:RWKV_W]
        ctx['k'] = ps[:, RWKV_W:2 * RWKV_W]
        ctx['v'] = ps[:, 2 * RWKV_W:3 * RWKV_W]
        z = ps[:, LORA_Z0:GATE_Z0]
        lane = lax.broadcasted_iota(jnp.int32, z.shape, 1)
        ctx['zt'] = jnp.where(lane < 64, jnp.tanh(z), z).astype(BF16)
        ctx['zg'] = _sigmoid(ps[:, GATE_Z0:SHIFT_W]).astype(BF16)

    def lora():
        ctx['wa'] = _dot(ctx['zt'], w2)

    def decay():
        wa = ctx['wa']
        ctx['logw'] = -DECAY_SCALE * _sigmoid(w0 + wa[:, :RWKV_W])
        ctx['a'] = _sigmoid(a0 + wa[:, RWKV_W:])

    def gate():
        ctx['g'] = _dot(ctx['zg'], gl_up)

    def key_norm():
        kk = ctx['k'] * k_k
        ctx['kk'] = kk
        ctx['kk_ss'] = _head_sum(kk * kk, bones)

    def keys():
        ctx['kk'] = ctx['kk'] * lax.rsqrt(ctx['kk_ss'] + NORM_EPS)
        ctx['k2'] = ctx['k'] * (1.0 + (ctx['a'] - 1.0) * k_a)

    return [split, lora, decay, gate, key_norm, keys]


def _bonus(r, k2, v, r_k, bones):
    return _head_sum(r * k2 * r_k, bones) * v


def _postlude_stages(ctx, ln_g, ln_b, bones):
    def mean():
        ctx['yc'] = ctx['y'] - _head_sum(ctx['y'], bones) * (1.0 / HEAD_DIM)

    def var():
        ctx['var'] = _head_sum(ctx['yc'] * ctx['yc'], bones) * (1.0 / HEAD_DIM)

    def gate():
        yn = ctx['yc'] * lax.rsqrt(ctx['var'] + GN_EPS) * ln_g + ln_b
        ctx['out'] = (yn + ctx['bonus']) * ctx['g']

    return [mean, var, gate]


def _run(stages):
    for stage in stages:
        stage()


def _rwkv_prelude(ps, *params):
    ctx = {'ps': ps}
    _run(_prelude_stages(ctx, *params))
    return tuple(ctx[name] for name in ('r', 'k2', 'v', 'logw', 'a', 'g', 'kk'))


def _rwkv_postlude(y, r, k2, v, g, r_k, ln_g, ln_b, bones):
    ctx = {'y': y, 'bonus': _bonus(r, k2, v, r_k, bones), 'g': g}
    _run(_postlude_stages(ctx, ln_g, ln_b, bones))
    return ctx['out']


def _inproj_pool_kernel(x_ref, g1_ref, win_ref, pw_ref, ps_ref, mu_ref, wout_ref, wup_ref, wdn_ref,
                        sh_ref, po_ref, plast_ref, ulast_ref, wout_bf_ref, wup_bf_ref, wdn_bf_ref,
                        ext_ref, prow_ref):
    tb = pl.program_id(1)
    rows = x_ref.shape[0]

    wout_bf_ref[...] = wout_ref[...].astype(BF16)
    wup_bf_ref[...] = wup_ref[...].astype(BF16)
    wdn_bf_ref[...] = wdn_ref[...].astype(BF16)

    @pl.when(tb == 0)
    def _():
        ext_ref[0:POOL_HALO, :] = jnp.zeros((POOL_HALO, POOL_W), F32)
        prow_ref[...] = jnp.zeros_like(prow_ref)

    h = _rmsnorm(x_ref[...], g1_ref[...]).astype(BF16)
    proj = _dot(h, win_ref[...])
    u = proj[:, :POOL_W]
    ext_ref[POOL_HALO:POOL_HALO + rows, :] = u
    ulast_ref[...] = u[rows - POOL_HALO:, :]

    pf = proj[:, POOL_W:]
    first = lax.broadcasted_iota(jnp.int32, (rows, 1), 0) == 0
    prev = jnp.where(first, prow_ref[0:1, :], pltpu.roll(pf, 1, 0))
    sh_ref[...] = _token_shift(pf, prev, mu_ref[...])
    prow_ref[0:1, :] = pf[rows - 1:rows, :]
    plast_ref[...] = jnp.broadcast_to(pf[rows - 1:rows, :], plast_ref.shape)

    pos = tb * rows + lax.broadcasted_iota(jnp.int32, (rows, 1), 0)
    outs = []
    for gi, win in enumerate(POOL_WINDOWS):
        cols = slice(gi * POOL_GC, (gi + 1) * POOL_GC)
        acc = ext_ref[:, cols]
        shift = 1
        while shift < win:
            acc = acc + pltpu.roll(acc, shift, 0)
            shift *= 2
        acc = acc[POOL_HALO:, :]
        cnt = jnp.minimum(win, pos + 1).astype(F32)
        d = acc / cnt - u[:, cols]
        outs.append(_dot(d.astype(BF16), pw_ref[gi]))
    po_ref[...] = (jnp.concatenate(outs, axis=1) * ps_ref[...]).astype(po_ref.dtype)
    ext_ref[0:POOL_HALO, :] = ext_ref[rows:rows + POOL_HALO, :]


def _inproj_pool(x, g1, win, pw, ps, mu, later_weights):
    b, t, _ = x.shape
    rows = ROW_BLOCK
    nt = t // rows
    steps = b * nt
    const = lambda *shape: pl.BlockSpec(shape, lambda i, j: (0,) * len(shape), pipeline_mode=pl.Buffered(1))
    slab = lambda w: pl.BlockSpec((w.shape[0] // steps, w.shape[1]), lambda i, j: (i * nt + j, 0))
    return pl.pallas_call(
        _inproj_pool_kernel,
        grid=(b, nt),
        in_specs=[
            pl.BlockSpec((None, rows, D_MODEL), lambda i, j: (i, j, 0)),
            const(1, D_MODEL),
            const(D_MODEL, IN_W),
            const(len(POOL_WINDOWS), POOL_GC, POOL_GC),
            const(1, POOL_W),
            const(1, SHIFT_W),
        ] + [slab(w) for w in later_weights],
        out_specs=[
            pl.BlockSpec((None, rows, SHIFT_W), lambda i, j: (i, j, 0)),
            pl.BlockSpec((None, rows, POOL_W), lambda i, j: (i, j, 0)),
            pl.BlockSpec((None, 8, SHIFT_W), lambda i, j: (i, 0, 0)),
            pl.BlockSpec((None, POOL_HALO, POOL_W), lambda i, j: (i, 0, 0)),
        ] + [slab(w) for w in later_weights],
        out_shape=[
            jax.ShapeDtypeStruct((b, t, SHIFT_W), F32),
            jax.ShapeDtypeStruct((b, t, POOL_W), BF16),
            jax.ShapeDtypeStruct((b, 8, SHIFT_W), F32),
            jax.ShapeDtypeStruct((b, POOL_HALO, POOL_W), F32),
        ] + [jax.ShapeDtypeStruct(w.shape, BF16) for w in later_weights],
        scratch_shapes=[pltpu.VMEM((rows + POOL_HALO, POOL_W), F32), pltpu.VMEM((8, SHIFT_W), F32)],
        compiler_params=pltpu.CompilerParams(
            dimension_semantics=("arbitrary", "arbitrary"), vmem_limit_bytes=VMEM_LIMIT),
        name="inproj_pool",
    )(x, g1, win, pw, ps, mu, *later_weights)


def _expand(x_bf, half):
    zero = jnp.zeros((x_bf.shape[0], LANES), x_bf.dtype)
    blocks = []
    for h in range(GROUP_HEADS):
        tile = h // 2
        piece = x_bf[:, tile * LANES:(tile + 1) * LANES] * half[h % 2]
        blocks.append(jnp.concatenate([piece, zero] if tile == 0 else [zero, piece], axis=1))
    return jnp.concatenate(blocks, axis=0)


def _diag_blocks(full, low):
    tiles = []
    for tile in range(GROUP_HEADS // 2):
        cols = slice(tile * LANES, (tile + 1) * LANES)
        left = full[(2 * tile) * HEAD_DIM:(2 * tile + 1) * HEAD_DIM, cols]
        right = full[(2 * tile + 1) * HEAD_DIM:(2 * tile + 2) * HEAD_DIM, cols]
        tiles.append(jnp.where(low, left, right))
    return jnp.concatenate(tiles, axis=1)


def _chunk_stages(ctx, strict, incl, iden, half):
    c = CHUNK
    each = lambda fn, *names: [fn(*args) for args in zip(*(ctx[name] for name in names))]
    bf = lambda x: x.astype(BF16)
    blockdiag = lambda x: _expand(bf(x), half)
    rows2 = lambda x, y: jnp.concatenate([x, y], axis=0)

    def scores():
        ctx['ar'] = each(rows2, 'at', 'rt')
        e_bk = each(lambda b_, k_: rows2(_expand(b_, half), _expand(k_, half)), 'bt', 'kt')
        ctx['sc'] = [_dot_nt(a_, e_) for a_, e_ in zip(ctx['ar'], e_bk)]

    def masks():
        ctx['lm_k'] = each(lambda x: bf(rows2(x[:c, GROUP_W:] * strict, x[c:, GROUP_W:] * incl)), 'sc')
        ctx['m_rb'] = each(lambda x: bf(x[c:, :GROUP_W] * incl), 'sc')
        ctx['p'] = each(lambda x: x[:c, :GROUP_W] * strict, 'sc')
        ctx['n'] = each(lambda x: iden + x, 'p')
        del ctx['sc']

    def square():
        ctx['p'] = each(lambda x: _dot(bf(x), blockdiag(x)), 'p')

    def double():
        out = each(lambda p_, n_: _dot(bf(rows2(p_, n_)), blockdiag(p_)), 'p', 'n')
        ctx['p'] = [x[:c] for x in out]
        ctx['n'] = [n_ + x[c:] for n_, x in zip(ctx['n'], out)]

    def inverse():
        ctx['t_inv'] = each(lambda n_, p_: bf(n_ + _dot(bf(n_), blockdiag(p_))), 'n', 'p')

    def values():
        ctx['lv'] = each(lambda m_, v_: _dot(m_, _expand(v_, half)), 'lm_k', 'vb')

    def state_in():
        ctx['ps'] = each(lambda a_, s_: _dot_nt(a_, blockdiag(s_)), 'ar', 's0')

    def solve():
        ctx['ub'] = each(lambda t_, p_, l_: bf(_dot(t_, blockdiag(p_[:c] + l_[:c]))), 't_inv', 'ps', 'lv')

    def outputs():
        ctx['y'] = each(lambda p_, m_, u_, l_: p_[c:] + _dot(m_, _expand(u_, half)) + l_[c:], 'ps', 'm_rb', 'ub', 'lv')

    def state_out():
        low = lax.broadcasted_iota(jnp.int32, (HEAD_DIM, LANES), 1) < HEAD_DIM
        full = each(lambda u_, v_, b_, k_: lax.dot_general(rows2(u_, v_), rows2(b_, k_), (((0,), (0,)), ((), ())),
                                                           preferred_element_type=F32), 'ub', 'vb', 'bt', 'kt')
        ctx['s_new'] = [(s_ + _diag_blocks(f_, low)) * w_ for s_, f_, w_ in zip(ctx['s0'], full, ctx['w_last'])]

    levels = int(np.log2(c))
    return ([scores, masks, square] + [double] * (levels - 2)
            + [inverse, values, state_in, solve, outputs, state_out])


def _rwkv_chunk_kernel(sh_ref, w0_ref, a0_ref, kk_ref, ka_ref, rk_ref, lng_ref, lnb_ref,
                       w2_ref, glup_ref, bones_ref, strict_ref, incl_ref, iden_ref, half_ref, tri_ref,
                       rw_ref, sfin_ref, s_scr):
    t = pl.program_id(1)
    nb, c, _ = sh_ref.shape
    seqs = range(nb)

    @pl.when(t == 0)
    def _():
        s_scr[...] = jnp.zeros_like(s_scr)

    bones = bones_ref[...]
    pre_params = (w0_ref[...], a0_ref[...], kk_ref[...], ka_ref[...], w2_ref[...], glup_ref[...], bones)
    chain_consts = (strict_ref[...], incl_ref[...], iden_ref[...], half_ref[...])
    tri = tri_ref[...]
    groups = [slice(gi * GROUP_W, (gi + 1) * GROUP_W) for gi in range(N_GROUPS)]
    rows = [slice(n * c, (n + 1) * c) for n in seqs]
    ctx = {'ps': jnp.concatenate([sh_ref[n] for n in seqs], axis=0)}

    def cumulative_decay():
        logw = ctx['logw']
        hi = logw.astype(BF16)
        lo = (logw - hi.astype(F32)).astype(BF16)
        ctx['cl'] = [_dot(tri, jnp.concatenate([hi[rs], lo[rs]], axis=0)) for rs in rows]

    def scale():
        aa = -ctx['kk']
        bb = ctx['kk'] * ctx['a']
        chain = {name: [] for name in ('at', 'rt', 'bt', 'kt', 'vb', 'w_last')}
        for rs, cl in zip(rows, ctx['cl']):
            w_in = jnp.exp(cl)
            w_ex = jnp.exp(cl - ctx['logw'][rs])
            w_inv = jnp.exp(-cl)
            for sl in groups:
                chain['at'].append((aa[rs, sl] * w_ex[:, sl]).astype(BF16))
                chain['rt'].append((ctx['r'][rs, sl] * w_in[:, sl]).astype(BF16))
                chain['bt'].append((bb[rs, sl] * w_inv[:, sl]).astype(BF16))
                chain['kt'].append((ctx['k2'][rs, sl] * w_inv[:, sl]).astype(BF16))
                chain['vb'].append(ctx['v'][rs, sl].astype(BF16))
                chain['w_last'].append(w_in[c - 1:c, sl])
        ctx.update(chain)
        ctx['s0'] = [s_scr[n, :, sl] for n in seqs for sl in groups]

    def store_state():
        for n in seqs:
            for gi, sl in enumerate(groups):
                s_scr[n, :, sl] = ctx['s_new'][n * N_GROUPS + gi]
        ctx['y'] = jnp.concatenate([jnp.concatenate(ctx['y'][n * N_GROUPS:(n + 1) * N_GROUPS], axis=1)
                                    for n in seqs], axis=0)

    def bonus():
        ctx['bonus'] = _bonus(ctx['r'], ctx['k2'], ctx['v'], rk_ref[...], bones)

    def store_out():
        rw_ref[...] = ctx['out'].reshape(nb, c, RWKV_W).astype(rw_ref.dtype)

    _run(_prelude_stages(ctx, *pre_params) + [cumulative_decay, scale]
         + _chunk_stages(ctx, *chain_consts) + [store_state, bonus]
         + _postlude_stages(ctx, lng_ref[...], lnb_ref[...], bones) + [store_out])

    @pl.when(t == pl.num_programs(1) - 1)
    def _():
        sfin_ref[...] = s_scr[...]


def _chunk_constants():
    c = CHUNK
    t = np.arange(c)[:, None]
    s = (np.arange(GROUP_HEADS * c) % c)[None, :]
    strict = (s < t).astype(np.float32)
    incl = (s <= t).astype(np.float32)
    iden = (s == t).astype(np.float32)
    lane_low = np.broadcast_to(np.arange(LANES)[None, :] < HEAD_DIM, (c, LANES))
    half = np.stack([lane_low, ~lane_low]).astype(np.float32)
    tri = (np.arange(c)[None, :] <= np.arange(c)[:, None]).astype(np.float32)
    tri = np.concatenate([tri, tri], axis=1)
    return (jnp.asarray(strict), jnp.asarray(incl), jnp.asarray(iden),
            jnp.asarray(half, dtype=BF16), jnp.asarray(tri, dtype=BF16))


def _head_ones():
    h = np.arange(GROUP_W) // HEAD_DIM
    return jnp.asarray((h[:, None] == h[None, :]).astype(np.float32), dtype=BF16)


def _rwkv_chunked(p, rows_prm, w2, gl_up, bones):
    b, t, _ = p.shape
    c = CHUNK
    nb = SEQ_BLOCK
    consts = _chunk_constants()
    const2 = lambda arr: pl.BlockSpec(arr.shape, lambda i, j: (0,) * arr.ndim)
    operands = list(rows_prm) + [w2, gl_up, bones] + list(consts)
    return pl.pallas_call(
        _rwkv_chunk_kernel,
        grid=(b // nb, t // c),
        in_specs=[pl.BlockSpec((nb, c, SHIFT_W), lambda i, j: (i, j, 0))] + [const2(a) for a in operands],
        out_specs=[
            pl.BlockSpec((nb, c, RWKV_W), lambda i, j: (i, j, 0)),
            pl.BlockSpec((nb, HEAD_DIM, RWKV_W), lambda i, j: (i, 0, 0)),
        ],
        out_shape=[
            jax.ShapeDtypeStruct((b, t, RWKV_W), BF16),
            jax.ShapeDtypeStruct((b, HEAD_DIM, RWKV_W), F32),
        ],
        scratch_shapes=[pltpu.VMEM((nb, HEAD_DIM, RWKV_W), F32)],
        compiler_params=pltpu.CompilerParams(
            dimension_semantics=("arbitrary", "arbitrary"), vmem_limit_bytes=VMEM_LIMIT),
        name="rwkv_chunked",
    )(p, *operands)


def _outproj_mlp_kernel(x_ref, po_ref, rw_ref, wout_ref, g2_ref, wup_ref, wdn_ref, gf_ref, y_ref):
    mix = _dot(po_ref[...], wout_ref[0:POOL_W, :]) + _dot(rw_ref[...], wout_ref[POOL_W:, :])
    x1 = x_ref[...] + mix
    h2 = _rmsnorm(x1, g2_ref[...]).astype(BF16)
    acc = jnp.zeros_like(x1)
    for j in range(D_FF // FF_BLOCK):
        cols = slice(j * FF_BLOCK, (j + 1) * FF_BLOCK)
        hid = jnp.maximum(_dot(h2, wup_ref[:, cols]), 0.0)
        acc = acc + _dot((hid * hid).astype(BF16), wdn_ref[cols, :])
    y_ref[...] = _rmsnorm(x1 + acc, gf_ref[...])


def _outproj_mlp(x, po, rw, wout, g2, wup, wdn, gf):
    n = x.shape[0]
    rows = min(MLP_ROW_BLOCK, n)
    const = lambda *shape: pl.BlockSpec(shape, lambda i: (0,) * len(shape), pipeline_mode=pl.Buffered(1))
    return pl.pallas_call(
        _outproj_mlp_kernel,
        grid=(n // rows,),
        in_specs=[
            pl.BlockSpec((rows, D_MODEL), lambda i: (i, 0)),
            pl.BlockSpec((rows, POOL_W), lambda i: (i, 0)),
            pl.BlockSpec((rows, RWKV_W), lambda i: (i, 0)),
            const(D_MODEL, D_MODEL),
            const(1, D_MODEL),
            const(D_MODEL, D_FF),
            const(D_FF, D_MODEL),
            const(1, D_MODEL),
        ],
        out_specs=pl.BlockSpec((rows, D_MODEL), lambda i: (i, 0)),
        out_shape=jax.ShapeDtypeStruct((n, D_MODEL), F32),
        compiler_params=pltpu.CompilerParams(
            dimension_semantics=("arbitrary",), vmem_limit_bytes=VMEM_LIMIT),
        name="outproj_mlp",
    )(x, po, rw, wout, g2, wup, wdn, gf)


def _sample_front_kernel(x_ref, sp_ref, sh_ref, g1_ref, win_ref, pw_ref, ps_ref, mu_ref, w0_ref, a0_ref,
                         kk_ref, ka_ref, w2_ref, glup_ref, bones_ref,
                         u_ref, p_ref, po_ref, tvec_ref, vec_ref):
    h = _rmsnorm(x_ref[...], g1_ref[...]).astype(BF16)
    proj = _dot(h, win_ref[...])
    u = proj[:, :POOL_W]
    pf = proj[:, POOL_W:]
    u_ref[...] = u
    p_ref[...] = pf

    outs = []
    for gi, win in enumerate(POOL_WINDOWS):
        cols = slice(gi * POOL_GC, (gi + 1) * POOL_GC)
        acc = u[:, cols]
        for j in range(1, win):
            acc = acc + sp_ref[POOL_BUF - j, :, cols]
        d = acc * (1.0 / win) - u[:, cols]
        outs.append(_dot(d.astype(BF16), pw_ref[gi]))
    po_ref[...] = (jnp.concatenate(outs, axis=1) * ps_ref[...]).astype(po_ref.dtype)

    bones = bones_ref[...]
    r, k2, v, logw, a, g, kk = _rwkv_prelude(
        _token_shift(pf, sh_ref[...], mu_ref[...]), w0_ref[...], a0_ref[...], kk_ref[...], ka_ref[...],
        w2_ref[...], glup_ref[...], bones)
    w = jnp.exp(logw)
    aa = -kk
    bb = kk * a
    q = w * r + aa * _head_sum(bb * r, bones)
    yv = v * _head_sum(k2 * r, bones)
    for i, x in enumerate((aa, q, w, bb, k2, v)):
        tvec_ref[i] = x.T
    for i, x in enumerate((yv, r, k2, v, g)):
        vec_ref[i] = x


def _sample_front(x, sp, sh, g1, win, pw, ps, rows_prm, w2, gl_up, bones):
    n = x.shape[0]
    mu, w0, a0, k_k, k_a = rows_prm
    return pl.pallas_call(
        _sample_front_kernel,
        out_shape=[
            jax.ShapeDtypeStruct((n, POOL_W), F32),
            jax.ShapeDtypeStruct((n, SHIFT_W), F32),
            jax.ShapeDtypeStruct((n, POOL_W), BF16),
            jax.ShapeDtypeStruct((6, RWKV_W, n), F32),
            jax.ShapeDtypeStruct((5, n, RWKV_W), F32),
        ],
        compiler_params=pltpu.CompilerParams(vmem_limit_bytes=VMEM_LIMIT),
        name="sample_front",
    )(x, sp, sh, g1, win, pw, ps, mu, w0, a0, k_k, k_a, w2, gl_up, bones)


def _decode_state_kernel(s_ref, tvec_ref, snew_ref, yt_ref):
    rows = s_ref.shape[0]
    v0 = pl.program_id(1) * rows
    aa = tvec_ref[0]
    q = tvec_ref[1]
    w = tvec_ref[2]
    bb = tvec_ref[3]
    k = tvec_ref[4]
    for i in range(rows):
        s0 = s_ref[i]
        v_row = tvec_ref[5, pl.ds(v0 + i, 1), :]
        sa = jnp.sum(s0 * aa, axis=0, keepdims=True)
        yt_ref[i:i + 1, :] = jnp.sum(s0 * q, axis=0, keepdims=True)
        snew_ref[i] = s0 * w + sa * bb + v_row * k


def _decode_state(s, tvec):
    n = s.shape[-1]
    rows = DEC_ROWS
    return pl.pallas_call(
        _decode_state_kernel,
        grid=(N_HEADS, HEAD_DIM // rows),
        in_specs=[
            pl.BlockSpec((None, rows, HEAD_DIM, n), lambda h, j: (h, j, 0, 0)),
            pl.BlockSpec((6, None, HEAD_DIM, n), lambda h, j: (0, h, 0, 0)),
        ],
        out_specs=[
            pl.BlockSpec((None, rows, HEAD_DIM, n), lambda h, j: (h, j, 0, 0)),
            pl.BlockSpec((None, rows, n), lambda h, j: (h, j, 0)),
        ],
        out_shape=[
            jax.ShapeDtypeStruct(s.shape, F32),
            jax.ShapeDtypeStruct((N_HEADS, HEAD_DIM, n), F32),
        ],
        compiler_params=pltpu.CompilerParams(
            dimension_semantics=("arbitrary", "arbitrary"), vmem_limit_bytes=VMEM_LIMIT),
        name="decode_state",
    )(s, tvec)


def _sample_post_kernel(y0t_ref, vec_ref, rk_ref, lng_ref, lnb_ref, bones_ref, rw_ref):
    y = y0t_ref[...].T + vec_ref[0]
    out = _rwkv_postlude(y, vec_ref[1], vec_ref[2], vec_ref[3], vec_ref[4],
                         rk_ref[...], lng_ref[...], lnb_ref[...], bones_ref[...])
    rw_ref[...] = out.astype(rw_ref.dtype)


def _sample_post(y0t, vec, r_k, ln_g, ln_b, bones):
    n = y0t.shape[1]
    return pl.pallas_call(
        _sample_post_kernel,
        out_shape=jax.ShapeDtypeStruct((n, RWKV_W), BF16),
        compiler_params=pltpu.CompilerParams(vmem_limit_bytes=VMEM_LIMIT),
        name="sample_post",
    )(y0t, vec, r_k, ln_g, ln_b, bones)


def kernel(x_prompt, x_sample, state_wkv, state_shift, state_pool, norm1_g, w_in, shift_mu, pool_w, pool_scale,
           w0, w_lora_up, a0, a_lora_up, g_lora_up, k_k, k_a, r_k, ln_x_g, ln_x_b, w_out, norm2_g, w_up, w_down,
           norm_f_g):
    depth = w_in.shape[0]
    assert depth == 1, "single-layer step"
    b, t, _ = x_prompt.shape
    n = x_sample.shape[0]
    row = lambda a: a.reshape(1, -1).astype(F32)

    g1 = row(norm1_g[0])
    win = w_in[0].astype(BF16)
    pw = pool_w[0].astype(BF16)
    ps = row(pool_scale[0])
    zeros = jnp.zeros((64, RWKV_W), F32)
    w2 = jnp.concatenate([jnp.concatenate([w_lora_up[0], zeros], axis=1),
                          jnp.concatenate([zeros, a_lora_up[0]], axis=1)], axis=0).astype(BF16)
    gl_up = g_lora_up[0].astype(BF16)
    bones = _head_ones()
    mu = row(shift_mu[0])
    pre_rows = (row(w0[0]), row(a0[0]), row(k_k[0]), row(k_a[0]))
    post_rows = (row(r_k[0]), row(ln_x_g[0]), row(ln_x_b[0]))
    g2 = row(norm2_g[0])
    gf = row(norm_f_g)

    sh_p, po_p, plast, ulast, wout, wup, wdn = _inproj_pool(x_prompt, g1, win, pw, ps, mu,
                                                            (w_out[0], w_up[0], w_down[0]))
    rw_p, s_nat = _rwkv_chunked(sh_p, pre_rows + post_rows, w2, gl_up, bones)
    y_p = _outproj_mlp(x_prompt.reshape(b * t, D_MODEL), po_p.reshape(b * t, POOL_W),
                       rw_p.reshape(b * t, RWKV_W), wout, g2, wup, wdn, gf).reshape(b, t, D_MODEL)
    wkv_p = jnp.transpose(s_nat.reshape(b, HEAD_DIM, N_HEADS, HEAD_DIM), (0, 2, 1, 3))[None]
    shift_p = plast[:, 0:1, :][None]
    pool_p = ulast[:, POOL_HALO - POOL_BUF:, :][None]

    xs = x_sample.reshape(n, D_MODEL)
    sp = state_pool[0]
    u_s, p_s, po_s, tvec, vec = _sample_front(xs, jnp.transpose(sp, (1, 0, 2)), state_shift[0].reshape(n, SHIFT_W),
                                              g1, win, pw, ps, (mu,) + pre_rows, w2, gl_up, bones)
    s_t = jnp.transpose(state_wkv[0], (1, 2, 3, 0))
    s_new_t, y0t = _decode_state(s_t, tvec.reshape(6, N_HEADS, HEAD_DIM, n))
    s_new = jnp.transpose(s_new_t, (3, 0, 1, 2))
    rw_s = _sample_post(y0t.reshape(RWKV_W, n), vec, *post_rows, bones)
    y_s = _outproj_mlp(xs, po_s, rw_s, wout, g2, wup, wdn, gf).reshape(n, 1, D_MODEL)
    shift_s = p_s.reshape(1, n, 1, SHIFT_W)
    pool_s = jnp.concatenate([sp[:, 1:, :], u_s[:, None, :]], axis=1)[None]

    return (y_p, y_s, wkv_p, shift_p, pool_p, s_new[None], shift_s, pool_s)
```

```python
import jax
import jax.numpy as jnp
import numpy as np
from jax import lax
from jax.experimental import pallas as pl
from jax.experimental.pallas import tpu as pltpu

F32 = jnp.float32
BF16 = jnp.bfloat16

D_MODEL = 1024
POOL_W = 512
RWKV_W = 512
POOL_WINDOWS = (2, 4, 8, 16)
POOL_GC = 128
POOL_BUF = 15
HEAD_DIM = 64
N_HEADS = 8
SHIFT_W = 1792
IN_W = POOL_W + SHIFT_W
D_FF = 4096
LORA_Z0 = 3 * RWKV_W
GATE_Z0 = LORA_Z0 + 128
RMS_EPS = 1e-6
GN_EPS = 64e-5
NORM_EPS = 1e-12
LOG2E = float(np.log2(np.e))
DECAY_SCALE = float(np.exp(-0.5)) * LOG2E

LANES = 128
CHUNK = 64
SEQ_BLOCK = 8
GROUP_HEADS = 4
GROUP_W = GROUP_HEADS * HEAD_DIM
N_GROUPS = N_HEADS // GROUP_HEADS
POOL_HALO = 16
ROW_BLOCK = 1024
MLP_ROW_BLOCK = 1024
FF_BLOCK = 1024
DEC_ROWS = 64
VMEM_LIMIT = 56 * 1024 * 1024


def _dot(a, b):
    return jnp.dot(a, b, preferred_element_type=F32)


def _dot_nt(a, b):
    return lax.dot_general(a, b, (((1,), (1,)), ((), ())), preferred_element_type=F32)


def _rmsnorm(x, g):
    return x * lax.rsqrt(jnp.mean(x * x, axis=-1, keepdims=True) + RMS_EPS) * g


def _sigmoid(x):
    return 1.0 / (1.0 + jnp.exp2(x * -LOG2E))


def _head_sum(x, bones):
    parts = [_dot(x[:, g * GROUP_W:(g + 1) * GROUP_W].astype(BF16), bones) for g in range(N_GROUPS)]
    return jnp.concatenate(parts, axis=1)


def _token_shift(pf, prev, mu):
    return pf + (prev - pf) * mu


def _prelude_stages(ctx, w0, a0, k_k, k_a, w2, gl_up, bones):
    def split():
        ps = ctx['ps']
        ctx['r'] = ps[:, 0:RWKV_W]
        ctx['k'] = ps[:, RWKV_W:2 * RWKV_W]
        ctx['v'] = ps[:, 2 * RWKV_W:3 * RWKV_W]
        z = ps[:, LORA_Z0:GATE_Z0]
        lane = lax.broadcasted_iota(jnp.int32, z.shape, 1)
        ctx['zt'] = jnp.where(lane < 64, jnp.tanh(z), z).astype(BF16)
        ctx['zg'] = _sigmoid(ps[:, GATE_Z0:SHIFT_W]).astype(BF16)

    def lora():
        ctx['wa'] = _dot(ctx['zt'], w2)

    def decay():
        wa = ctx['wa']
        ctx['logw'] = -DECAY_SCALE * _sigmoid(w0 + wa[:, :RWKV_W])
        ctx['a'] = _sigmoid(a0 + wa[:, RWKV_W:])

    def gate():
        ctx['g'] = _dot(ctx['zg'], gl_up)

    def key_norm():
        kk = ctx['k'] * k_k
        ctx['kk'] = kk
        ctx['kk_ss'] = _head_sum(kk * kk, bones)

    def keys():
        ctx['kk'] = ctx['kk'] * lax.rsqrt(ctx['kk_ss'] + NORM_EPS)
        ctx['k2'] = ctx['k'] * (ctx['a'] * k_a[0:1, :] + k_a[1:2, :])

    return [split, lora, decay, gate, key_norm, keys]


def _bonus(r, k2, v, r_k, bones):
    return _head_sum(r * k2 * r_k, bones) * v


def _postlude_stages(ctx, ln_g, ln_b, bones):
    def mean():
        ctx['yc'] = ctx['y'] - _head_sum(ctx['y'], bones) * (1.0 / HEAD_DIM)

    def var():
        ctx['var'] = _head_sum(ctx['yc'] * ctx['yc'], bones) * (1.0 / HEAD_DIM)

    def gate():
        yn = ctx['yc'] * lax.rsqrt(ctx['var'] + GN_EPS) * ln_g + ln_b
        ctx['out'] = (yn + ctx['bonus']) * ctx['g']

    return [mean, var, gate]


def _run(stages):
    for stage in stages:
        stage()


def _rwkv_prelude(ps, *params):
    ctx = {'ps': ps}
    _run(_prelude_stages(ctx, *params))
    return tuple(ctx[name] for name in ('r', 'k2', 'v', 'logw', 'a', 'g', 'kk'))


def _rwkv_postlude(y, r, k2, v, g, r_k, ln_g, ln_b, bones):
    ctx = {'y': y, 'bonus': _bonus(r, k2, v, r_k, bones), 'g': g}
    _run(_postlude_stages(ctx, ln_g, ln_b, bones))
    return ctx['out']


def _inproj_pool_kernel(x_ref, g1_ref, win_ref, pw_ref, ps_ref, mu_ref, wout_ref, wup_ref, wdn_ref,
                        sh_ref, po_ref, plast_ref, ulast_ref, wout_bf_ref, wup_bf_ref, wdn_bf_ref,
                        ext_ref, prow_ref):
    tb = pl.program_id(1)
    rows = x_ref.shape[0]

    wout_bf_ref[...] = wout_ref[...].astype(BF16)
    wup_bf_ref[...] = wup_ref[...].astype(BF16)
    wdn_bf_ref[...] = wdn_ref[...].astype(BF16)

    @pl.when(tb == 0)
    def _():
        ext_ref[0:POOL_HALO, :] = jnp.zeros((POOL_HALO, POOL_W), F32)
        prow_ref[...] = jnp.zeros_like(prow_ref)

    h = _rmsnorm(x_ref[...], g1_ref[...]).astype(BF16)
    proj = _dot(h, win_ref[...])
    u = proj[:, :POOL_W]
    ext_ref[POOL_HALO:POOL_HALO + rows, :] = u
    ulast_ref[...] = u[rows - POOL_HALO:, :]

    pf = proj[:, POOL_W:]
    first = lax.broadcasted_iota(jnp.int32, (rows, 1), 0) == 0
    prev = jnp.where(first, prow_ref[0:1, :], pltpu.roll(pf, 1, 0))
    sh_ref[...] = _token_shift(pf, prev, mu_ref[...])
    prow_ref[0:1, :] = pf[rows - 1:rows, :]
    plast_ref[...] = jnp.broadcast_to(pf[rows - 1:rows, :], plast_ref.shape)

    pos = tb * rows + lax.broadcasted_iota(jnp.int32, (rows, 1), 0)
    outs = []
    for gi, win in enumerate(POOL_WINDOWS):
        cols = slice(gi * POOL_GC, (gi + 1) * POOL_GC)
        acc = ext_ref[:, cols]
        shift = 1
        while shift < win:
            acc = acc + pltpu.roll(acc, shift, 0)
            shift *= 2
        acc = acc[POOL_HALO:, :]
        cnt = jnp.minimum(win, pos + 1).astype(F32)
        d = acc / cnt - u[:, cols]
        outs.append(_dot(d.astype(BF16), pw_ref[gi]))
    po_ref[...] = (jnp.concatenate(outs, axis=1) * ps_ref[...]).astype(po_ref.dtype)
    ext_ref[0:POOL_HALO, :] = ext_ref[rows:rows + POOL_HALO, :]


def _inproj_pool(x, g1, win, pw, ps, mu, later_weights):
    b, t, _ = x.shape
    rows = ROW_BLOCK
    nt = t // rows
    steps = b * nt
    const = lambda *shape: pl.BlockSpec(shape, lambda i, j: (0,) * len(shape), pipeline_mode=pl.Buffered(1))
    slab = lambda w: pl.BlockSpec((w.shape[0] // steps, w.shape[1]), lambda i, j: (i * nt + j, 0))
    return pl.pallas_call(
        _inproj_pool_kernel,
        grid=(b, nt),
        in_specs=[
            pl.BlockSpec((None, rows, D_MODEL), lambda i, j: (i, j, 0)),
            const(1, D_MODEL),
            const(D_MODEL, IN_W),
            const(len(POOL_WINDOWS), POOL_GC, POOL_GC),
            const(1, POOL_W),
            const(1, SHIFT_W),
        ] + [slab(w) for w in later_weights],
        out_specs=[
            pl.BlockSpec((None, rows, SHIFT_W), lambda i, j: (i, j, 0)),
            pl.BlockSpec((None, rows, POOL_W), lambda i, j: (i, j, 0)),
            pl.BlockSpec((None, 8, SHIFT_W), lambda i, j: (i, 0, 0)),
            pl.BlockSpec((None, POOL_HALO, POOL_W), lambda i, j: (i, 0, 0)),
        ] + [slab(w) for w in later_weights],
        out_shape=[
            jax.ShapeDtypeStruct((b, t, SHIFT_W), F32),
            jax.ShapeDtypeStruct((b, t, POOL_W), BF16),
            jax.ShapeDtypeStruct((b, 8, SHIFT_W), F32),
            jax.ShapeDtypeStruct((b, POOL_HALO, POOL_W), F32),
        ] + [jax.ShapeDtypeStruct(w.shape, BF16) for w in later_weights],
        scratch_shapes=[pltpu.VMEM((rows + POOL_HALO, POOL_W), F32), pltpu.VMEM((8, SHIFT_W), F32)],
        compiler_params=pltpu.CompilerParams(
            dimension_semantics=("arbitrary", "arbitrary"), vmem_limit_bytes=VMEM_LIMIT),
        name="inproj_pool",
    )(x, g1, win, pw, ps, mu, *later_weights)


def _expand(x_bf, half):
    zero = jnp.zeros((x_bf.shape[0], LANES), x_bf.dtype)
    blocks = []
    for h in range(GROUP_HEADS):
        tile = h // 2
        piece = x_bf[:, tile * LANES:(tile + 1) * LANES] * half[h % 2]
        blocks.append(jnp.concatenate([piece, zero] if tile == 0 else [zero, piece], axis=1))
    return jnp.concatenate(blocks, axis=0)


def _diag_blocks(full, low):
    tiles = []
    for tile in range(GROUP_HEADS // 2):
        cols = slice(tile * LANES, (tile + 1) * LANES)
        left = full[(2 * tile) * HEAD_DIM:(2 * tile + 1) * HEAD_DIM, cols]
        right = full[(2 * tile + 1) * HEAD_DIM:(2 * tile + 2) * HEAD_DIM, cols]
        tiles.append(jnp.where(low, left, right))
    return jnp.concatenate(tiles, axis=1)


def _chunk_stages(ctx, strict, incl, iden, half):
    c = CHUNK
    each = lambda fn, *names: [fn(*args) for args in zip(*(ctx[name] for name in names))]
    bf = lambda x: x.astype(BF16)
    blockdiag = lambda x: _expand(bf(x), half)
    rows2 = lambda x, y: jnp.concatenate([x, y], axis=0)

    def scores():
        ctx['ar'] = each(rows2, 'at', 'rt')
        e_bk = each(lambda b_, k_: rows2(_expand(b_, half), _expand(k_, half)), 'bt', 'kt')
        ctx['sc'] = [_dot_nt(a_, e_) for a_, e_ in zip(ctx['ar'], e_bk)]

    def masks():
        ctx['lm_k'] = each(lambda x: bf(rows2(x[:c, GROUP_W:] * strict, x[c:, GROUP_W:] * incl)), 'sc')
        ctx['m_rb'] = each(lambda x: bf(x[c:, :GROUP_W] * incl), 'sc')
        ctx['p'] = each(lambda x: x[:c, :GROUP_W] * strict, 'sc')
        ctx['n'] = each(lambda x: iden + x, 'p')
        del ctx['sc']

    def square():
        ctx['p'] = each(lambda x: _dot(bf(x), blockdiag(x)), 'p')

    def double():
        out = each(lambda p_, n_: _dot(bf(rows2(p_, n_)), blockdiag(p_)), 'p', 'n')
        ctx['p'] = [x[:c] for x in out]
        ctx['n'] = [n_ + x[c:] for n_, x in zip(ctx['n'], out)]

    def inverse():
        ctx['t_inv'] = each(lambda n_, p_: bf(n_ + _dot(bf(n_), blockdiag(p_))), 'n', 'p')

    def values():
        ctx['lv'] = each(lambda m_, v_: _dot(m_, _expand(v_, half)), 'lm_k', 'vb')

    def state_in():
        ctx['ps'] = each(lambda a_, s_: _dot_nt(a_, blockdiag(s_)), 'ar', 's0')

    def solve():
        ctx['ub'] = each(lambda t_, p_, l_: bf(_dot(t_, blockdiag(p_[:c] + l_[:c]))), 't_inv', 'ps', 'lv')

    def outputs():
        ctx['y'] = each(lambda p_, m_, u_, l_: p_[c:] + _dot(m_, _expand(u_, half)) + l_[c:], 'ps', 'm_rb', 'ub', 'lv')

    def state_out():
        low = lax.broadcasted_iota(jnp.int32, (HEAD_DIM, LANES), 1) < HEAD_DIM
        full = each(lambda u_, v_, b_, k_: lax.dot_general(rows2(u_, v_), rows2(b_, k_), (((0,), (0,)), ((), ())),
                                                           preferred_element_type=F32), 'ub', 'vb', 'bt', 'kt')
        ctx['s_new'] = [(s_ + _diag_blocks(f_, low)) * w_ for s_, f_, w_ in zip(ctx['s0'], full, ctx['w_last'])]

    levels = int(np.log2(c))
    return ([scores, masks, square] + [double] * (levels - 2)
            + [inverse, values, state_in, solve, outputs, state_out])


def _rwkv_chunk_kernel(sh_ref, w0_ref, a0_ref, kk_ref, ka_ref, rk_ref, lng_ref, lnb_ref,
                       w2_ref, glup_ref, bones_ref, strict_ref, incl_ref, iden_ref, half_ref, tri_ref,
                       rw_ref, sfin_ref, s_scr):
    t = pl.program_id(1)
    nb, c, _ = sh_ref.shape
    seqs = range(nb)

    @pl.when(t == 0)
    def _():
        s_scr[...] = jnp.zeros_like(s_scr)

    bones = bones_ref[...]
    pre_params = (w0_ref[...], a0_ref[...], kk_ref[...], ka_ref[...], w2_ref[...], glup_ref[...], bones)
    chain_consts = (strict_ref[...], incl_ref[...], iden_ref[...], half_ref[...])
    tri = tri_ref[...]
    groups = [slice(gi * GROUP_W, (gi + 1) * GROUP_W) for gi in range(N_GROUPS)]
    rows = [slice(n * c, (n + 1) * c) for n in seqs]
    ctx = {'ps': jnp.concatenate([sh_ref[n] for n in seqs], axis=0)}

    def cumulative_decay():
        logw = ctx['logw']
        hi = logw.astype(BF16)
        lo = (logw - hi.astype(F32)).astype(BF16)
        ctx['cl'] = [_dot(tri, jnp.concatenate([hi[rs], lo[rs]], axis=0)) for rs in rows]

    def scale():
        aa = -ctx['kk']
        bb = ctx['kk'] * ctx['a']
        chain = {name: [] for name in ('at', 'rt', 'bt', 'kt', 'vb', 'w_last')}
        for rs, cl in zip(rows, ctx['cl']):
            w_in = jnp.exp2(cl)
            w_ex = jnp.exp2(cl - ctx['logw'][rs])
            w_inv = jnp.exp2(-cl)
            for sl in groups:
                chain['at'].append((aa[rs, sl] * w_ex[:, sl]).astype(BF16))
                chain['rt'].append((ctx['r'][rs, sl] * w_in[:, sl]).astype(BF16))
                chain['bt'].append((bb[rs, sl] * w_inv[:, sl]).astype(BF16))
                chain['kt'].append((ctx['k2'][rs, sl] * w_inv[:, sl]).astype(BF16))
                chain['vb'].append(ctx['v'][rs, sl].astype(BF16))
                chain['w_last'].append(w_in[c - 1:c, sl])
        ctx.update(chain)
        ctx['s0'] = [s_scr[n, :, sl] for n in seqs for sl in groups]

    def store_state():
        for n in seqs:
            for gi, sl in enumerate(groups):
                s_scr[n, :, sl] = ctx['s_new'][n * N_GROUPS + gi]
        ctx['y'] = jnp.concatenate([jnp.concatenate(ctx['y'][n * N_GROUPS:(n + 1) * N_GROUPS], axis=1)
                                    for n in seqs], axis=0)

    def bonus():
        ctx['bonus'] = _bonus(ctx['r'], ctx['k2'], ctx['v'], rk_ref[...], bones)

    def store_out():
        rw_ref[...] = ctx['out'].reshape(nb, c, RWKV_W).astype(rw_ref.dtype)

    _run(_prelude_stages(ctx, *pre_params) + [cumulative_decay, scale]
         + _chunk_stages(ctx, *chain_consts) + [store_state, bonus]
         + _postlude_stages(ctx, lng_ref[...], lnb_ref[...], bones) + [store_out])

    @pl.when(t == pl.num_programs(1) - 1)
    def _():
        sfin_ref[...] = s_scr[...]


def _chunk_constants():
    c = CHUNK
    t = np.arange(c)[:, None]
    s = (np.arange(GROUP_HEADS * c) % c)[None, :]
    strict = (s < t).astype(np.float32)
    incl = (s <= t).astype(np.float32)
    iden = (s == t).astype(np.float32)
    lane_low = np.broadcast_to(np.arange(LANES)[None, :] < HEAD_DIM, (c, LANES))
    half = np.stack([lane_low, ~lane_low]).astype(np.float32)
    tri = (np.arange(c)[None, :] <= np.arange(c)[:, None]).astype(np.float32)
    tri = np.concatenate([tri, tri], axis=1)
    return (jnp.asarray(strict), jnp.asarray(incl), jnp.asarray(iden),
            jnp.asarray(half, dtype=BF16), jnp.asarray(tri, dtype=BF16))


def _head_ones():
    h = np.arange(GROUP_W) // HEAD_DIM
    return jnp.asarray((h[:, None] == h[None, :]).astype(np.float32), dtype=BF16)


def _rwkv_chunked(p, rows_prm, w2, gl_up, bones):
    b, t, _ = p.shape
    c = CHUNK
    nb = SEQ_BLOCK
    consts = _chunk_constants()
    const2 = lambda arr: pl.BlockSpec(arr.shape, lambda i, j: (0,) * arr.ndim)
    operands = list(rows_prm) + [w2, gl_up, bones] + list(consts)
    return pl.pallas_call(
        _rwkv_chunk_kernel,
        grid=(b // nb, t // c),
        in_specs=[pl.BlockSpec((nb, c, SHIFT_W), lambda i, j: (i, j, 0))] + [const2(a) for a in operands],
        out_specs=[
            pl.BlockSpec((nb, c, RWKV_W), lambda i, j: (i, j, 0)),
            pl.BlockSpec((nb, HEAD_DIM, RWKV_W), lambda i, j: (i, 0, 0)),
        ],
        out_shape=[
            jax.ShapeDtypeStruct((b, t, RWKV_W), BF16),
            jax.ShapeDtypeStruct((b, HEAD_DIM, RWKV_W), F32),
        ],
        scratch_shapes=[pltpu.VMEM((nb, HEAD_DIM, RWKV_W), F32)],
        compiler_params=pltpu.CompilerParams(
            dimension_semantics=("arbitrary", "arbitrary"), vmem_limit_bytes=VMEM_LIMIT),
        name="rwkv_chunked",
    )(p, *operands)


def _outproj_mlp_kernel(x_ref, po_ref, rw_ref, wout_ref, g2_ref, wup_ref, wdn_ref, gf_ref, y_ref):
    mix = _dot(po_ref[...], wout_ref[0:POOL_W, :]) + _dot(rw_ref[...], wout_ref[POOL_W:, :])
    x1 = x_ref[...] + mix
    h2 = _rmsnorm(x1, g2_ref[...]).astype(BF16)
    acc = jnp.zeros_like(x1)
    for j in range(D_FF // FF_BLOCK):
        cols = slice(j * FF_BLOCK, (j + 1) * FF_BLOCK)
        hid = jnp.maximum(_dot(h2, wup_ref[:, cols]), 0.0)
        acc = acc + _dot((hid * hid).astype(BF16), wdn_ref[cols, :])
    y_ref[...] = _rmsnorm(x1 + acc, gf_ref[...])


def _outproj_mlp(x, po, rw, wout, g2, wup, wdn, gf):
    n = x.shape[0]
    rows = min(MLP_ROW_BLOCK, n)
    const = lambda *shape: pl.BlockSpec(shape, lambda i: (0,) * len(shape), pipeline_mode=pl.Buffered(1))
    return pl.pallas_call(
        _outproj_mlp_kernel,
        grid=(n // rows,),
        in_specs=[
            pl.BlockSpec((rows, D_MODEL), lambda i: (i, 0)),
            pl.BlockSpec((rows, POOL_W), lambda i: (i, 0)),
            pl.BlockSpec((rows, RWKV_W), lambda i: (i, 0)),
            const(D_MODEL, D_MODEL),
            const(1, D_MODEL),
            const(D_MODEL, D_FF),
            const(D_FF, D_MODEL),
            const(1, D_MODEL),
        ],
        out_specs=pl.BlockSpec((rows, D_MODEL), lambda i: (i, 0)),
        out_shape=jax.ShapeDtypeStruct((n, D_MODEL), F32),
        compiler_params=pltpu.CompilerParams(
            dimension_semantics=("arbitrary",), vmem_limit_bytes=VMEM_LIMIT),
        name="outproj_mlp",
    )(x, po, rw, wout, g2, wup, wdn, gf)


def _sample_front_kernel(x_ref, sp_ref, sh_ref, g1_ref, win_ref, pw_ref, ps_ref, mu_ref, w0_ref, a0_ref,
                         kk_ref, ka_ref, w2_ref, glup_ref, bones_ref,
                         u_ref, p_ref, po_ref, tvec_ref, vec_ref):
    h = _rmsnorm(x_ref[...], g1_ref[...]).astype(BF16)
    proj = _dot(h, win_ref[...])
    u = proj[:, :POOL_W]
    pf = proj[:, POOL_W:]
    u_ref[...] = u
    p_ref[...] = pf

    outs = []
    for gi, win in enumerate(POOL_WINDOWS):
        cols = slice(gi * POOL_GC, (gi + 1) * POOL_GC)
        acc = u[:, cols]
        for j in range(1, win):
            acc = acc + sp_ref[POOL_BUF - j, :, cols]
        d = acc * (1.0 / win) - u[:, cols]
        outs.append(_dot(d.astype(BF16), pw_ref[gi]))
    po_ref[...] = (jnp.concatenate(outs, axis=1) * ps_ref[...]).astype(po_ref.dtype)

    bones = bones_ref[...]
    r, k2, v, logw, a, g, kk = _rwkv_prelude(
        _token_shift(pf, sh_ref[...], mu_ref[...]), w0_ref[...], a0_ref[...], kk_ref[...], ka_ref[...],
        w2_ref[...], glup_ref[...], bones)
    w = jnp.exp2(logw)
    aa = -kk
    bb = kk * a
    q = w * r + aa * _head_sum(bb * r, bones)
    yv = v * _head_sum(k2 * r, bones)
    for i, x in enumerate((aa, q, w, bb, k2, v)):
        tvec_ref[i] = x.T
    for i, x in enumerate((yv, r, k2, v, g)):
        vec_ref[i] = x


def _sample_front(x, sp, sh, g1, win, pw, ps, rows_prm, w2, gl_up, bones):
    n = x.shape[0]
    mu, w0, a0, k_k, k_a = rows_prm
    return pl.pallas_call(
        _sample_front_kernel,
        out_shape=[
            jax.ShapeDtypeStruct((n, POOL_W), F32),
            jax.ShapeDtypeStruct((n, SHIFT_W), F32),
            jax.ShapeDtypeStruct((n, POOL_W), BF16),
            jax.ShapeDtypeStruct((6, RWKV_W, n), F32),
            jax.ShapeDtypeStruct((5, n, RWKV_W), F32),
        ],
        compiler_params=pltpu.CompilerParams(vmem_limit_bytes=VMEM_LIMIT),
        name="sample_front",
    )(x, sp, sh, g1, win, pw, ps, mu, w0, a0, k_k, k_a, w2, gl_up, bones)


def _decode_state_kernel(s_ref, tvec_ref, snew_ref, yt_ref):
    rows = s_ref.shape[0]
    v0 = pl.program_id(1) * rows
    aa = tvec_ref[0]
    q = tvec_ref[1]
    w = tvec_ref[2]
    bb = tvec_ref[3]
    k = tvec_ref[4]
    for i in range(rows):
        s0 = s_ref[i]
        v_row = tvec_ref[5, pl.ds(v0 + i, 1), :]
        sa = jnp.sum(s0 * aa, axis=0, keepdims=True)
        yt_ref[i:i + 1, :] = jnp.sum(s0 * q, axis=0, keepdims=True)
        snew_ref[i] = s0 * w + sa * bb + v_row * k


def _decode_state(s, tvec):
    n = s.shape[-1]
    rows = DEC_ROWS
    return pl.pallas_call(
        _decode_state_kernel,
        grid=(N_HEADS, HEAD_DIM // rows),
        in_specs=[
            pl.BlockSpec((None, rows, HEAD_DIM, n), lambda h, j: (h, j, 0, 0)),
            pl.BlockSpec((6, None, HEAD_DIM, n), lambda h, j: (0, h, 0, 0)),
        ],
        out_specs=[
            pl.BlockSpec((None, rows, HEAD_DIM, n), lambda h, j: (h, j, 0, 0)),
            pl.BlockSpec((None, rows, n), lambda h, j: (h, j, 0)),
        ],
        out_shape=[
            jax.ShapeDtypeStruct(s.shape, F32),
            jax.ShapeDtypeStruct((N_HEADS, HEAD_DIM, n), F32),
        ],
        compiler_params=pltpu.CompilerParams(
            dimension_semantics=("arbitrary", "arbitrary"), vmem_limit_bytes=VMEM_LIMIT),
        name="decode_state",
    )(s, tvec)


def _sample_post_kernel(y0t_ref, vec_ref, rk_ref, lng_ref, lnb_ref, bones_ref, rw_ref):
    y = y0t_ref[...].T + vec_ref[0]
    out = _rwkv_postlude(y, vec_ref[1], vec_ref[2], vec_ref[3], vec_ref[4],
                         rk_ref[...], lng_ref[...], lnb_ref[...], bones_ref[...])
    rw_ref[...] = out.astype(rw_ref.dtype)


def _sample_post(y0t, vec, r_k, ln_g, ln_b, bones):
    n = y0t.shape[1]
    return pl.pallas_call(
        _sample_post_kernel,
        out_shape=jax.ShapeDtypeStruct((n, RWKV_W), BF16),
        compiler_params=pltpu.CompilerParams(vmem_limit_bytes=VMEM_LIMIT),
        name="sample_post",
    )(y0t, vec, r_k, ln_g, ln_b, bones)


def kernel(x_prompt, x_sample, state_wkv, state_shift, state_pool, norm1_g, w_in, shift_mu, pool_w, pool_scale,
           w0, w_lora_up, a0, a_lora_up, g_lora_up, k_k, k_a, r_k, ln_x_g, ln_x_b, w_out, norm2_g, w_up, w_down,
           norm_f_g):
    depth = w_in.shape[0]
    assert depth == 1, "single-layer step"
    b, t, _ = x_prompt.shape
    n = x_sample.shape[0]
    row = lambda a: a.reshape(1, -1).astype(F32)

    g1 = row(norm1_g[0])
    win = w_in[0].astype(BF16)
    pw = pool_w[0].astype(BF16)
    ps = row(pool_scale[0])
    zeros = jnp.zeros((64, RWKV_W), F32)
    w2 = jnp.concatenate([jnp.concatenate([w_lora_up[0], zeros], axis=1),
                          jnp.concatenate([zeros, a_lora_up[0]], axis=1)], axis=0).astype(BF16)
    gl_up = g_lora_up[0].astype(BF16)
    bones = _head_ones()
    mu = row(shift_mu[0])
    pre_rows = (row(w0[0]), row(a0[0]), row(k_k[0]), jnp.concatenate([row(k_a[0]), 1.0 - row(k_a[0])], axis=0))
    post_rows = (row(r_k[0]), row(ln_x_g[0]), row(ln_x_b[0]))
    g2 = row(norm2_g[0])
    gf = row(norm_f_g)

    sh_p, po_p, plast, ulast, wout, wup, wdn = _inproj_pool(x_prompt, g1, win, pw, ps, mu,
                                                            (w_out[0], w_up[0], w_down[0]))
    rw_p, s_nat = _rwkv_chunked(sh_p, pre_rows + post_rows, w2, gl_up, bones)
    y_p = _outproj_mlp(x_prompt.reshape(b * t, D_MODEL), po_p.reshape(b * t, POOL_W),
                       rw_p.reshape(b * t, RWKV_W), wout, g2, wup, wdn, gf).reshape(b, t, D_MODEL)
    wkv_p = jnp.transpose(s_nat.reshape(b, HEAD_DIM, N_HEADS, HEAD_DIM), (0, 2, 1, 3))[None]
    shift_p = plast[:, 0:1, :][None]
    pool_p = ulast[:, POOL_HALO - POOL_BUF:, :][None]

    xs = x_sample.reshape(n, D_MODEL)
    sp = state_pool[0]
    u_s, p_s, po_s, tvec, vec = _sample_front(xs, jnp.transpose(sp, (1, 0, 2)), state_shift[0].reshape(n, SHIFT_W),
                                              g1, win, pw, ps, (mu,) + pre_rows, w2, gl_up, bones)
    s_t = jnp.transpose(state_wkv[0], (1, 2, 3, 0))
    s_new_t, y0t = _decode_state(s_t, tvec.reshape(6, N_HEADS, HEAD_DIM, n))
    s_new = jnp.transpose(s_new_t, (3, 0, 1, 2))
    rw_s = _sample_post(y0t.reshape(RWKV_W, n), vec, *post_rows, bones)
    y_s = _outproj_mlp(xs, po_s, rw_s, wout, g2, wup, wdn, gf).reshape(n, 1, D_MODEL)
    shift_s = p_s.reshape(1, n, 1, SHIFT_W)
    pool_s = jnp.concatenate([sp[:, 1:, :], u_s[:, None, :]], axis=1)[None]

    return (y_p, y_s, wkv_p, shift_p, pool_p, s_new[None], shift_s, pool_s)
```

```python
import jax
import jax.numpy as jnp
import numpy as np
from jax import lax
from jax.experimental import pallas as pl
from jax.experimental.pallas import tpu as pltpu

F32 = jnp.float32
BF16 = jnp.bfloat16

D_MODEL = 1024
POOL_W = 512
RWKV_W = 512
POOL_WINDOWS = (2, 4, 8, 16)
POOL_GC = 128
POOL_BUF = 15
HEAD_DIM = 64
N_HEADS = 8
SHIFT_W = 1792
IN_W = POOL_W + SHIFT_W
D_FF = 4096
LORA_Z0 = 3 * RWKV_W
GATE_Z0 = LORA_Z0 + 128
RMS_EPS = 1e-6
GN_EPS = 64e-5
NORM_EPS = 1e-12
LOG2E = float(np.log2(np.e))
DECAY_SCALE = float(np.exp(-0.5)) * LOG2E

LANES = 128
CHUNK = 64
SEQ_BLOCK = 8
GROUP_HEADS = 4
GROUP_W = GROUP_HEADS * HEAD_DIM
N_GROUPS = N_HEADS // GROUP_HEADS
POOL_HALO = 16
ROW_BLOCK = 1024
MLP_ROW_BLOCK = 1024
FF_BLOCK = 1024
DEC_ROWS = 64
VMEM_LIMIT = 56 * 1024 * 1024


def _dot(a, b):
    return jnp.dot(a, b, preferred_element_type=F32)


def _dot_nt(a, b):
    return lax.dot_general(a, b, (((1,), (1,)), ((), ())), preferred_element_type=F32)


def _rmsnorm(x, g):
    return x * lax.rsqrt(jnp.mean(x * x, axis=-1, keepdims=True) + RMS_EPS) * g


def _sigmoid(x):
    return 1.0 / (1.0 + jnp.exp2(x * -LOG2E))


def _head_sum(x, bones):
    parts = [_dot(x[:, g * GROUP_W:(g + 1) * GROUP_W].astype(BF16), bones) for g in range(N_GROUPS)]
    return jnp.concatenate(parts, axis=1)


def _token_shift(pf, prev, mu):
    return pf + (prev - pf) * mu


def _prelude_stages(ctx, w0, a0, k_k, k_a, w2, gl_up, bones):
    def split():
        ps = ctx['ps']
        ctx['r'] = ps[:, 0:RWKV_W]
        ctx['k'] = ps[:, RWKV_W:2 * RWKV_W]
        ctx['v'] = ps[:, 2 * RWKV_W:3 * RWKV_W]
        z = ps[:, LORA_Z0:GATE_Z0]
        lane = lax.broadcasted_iota(jnp.int32, z.shape, 1)
        ctx['zt'] = jnp.where(lane < 64, jnp.tanh(z), z).astype(BF16)
        ctx['zg'] = _sigmoid(ps[:, GATE_Z0:SHIFT_W]).astype(BF16)

    def lora():
        ctx['wa'] = _dot(ctx['zt'], w2)

    def decay():
        wa = ctx['wa']
        ctx['logw'] = -DECAY_SCALE * _sigmoid(w0 + wa[:, :RWKV_W])
        ctx['a'] = _sigmoid(a0 + wa[:, RWKV_W:])

    def gate():
        ctx['g'] = _dot(ctx['zg'], gl_up)

    def key_norm():
        kk = ctx['k'] * k_k
        ctx['kk'] = kk
        ctx['kk_ss'] = _head_sum(kk * kk, bones)

    def keys():
        ctx['kk'] = ctx['kk'] * lax.rsqrt(ctx['kk_ss'] + NORM_EPS)
        ctx['k2'] = ctx['k'] * (ctx['a'] * k_a[0:1, :] + k_a[1:2, :])

    return [split, lora, decay, gate, key_norm, keys]


def _bonus(r, k2, v, r_k, bones):
    return _head_sum(r * k2 * r_k, bones) * v


def _postlude_stages(ctx, ln_g, ln_b, bones):
    def mean():
        ctx['yc'] = ctx['y'] - _head_sum(ctx['y'], bones) * (1.0 / HEAD_DIM)

    def var():
        ctx['var'] = _head_sum(ctx['yc'] * ctx['yc'], bones) * (1.0 / HEAD_DIM)

    def gate():
        yn = ctx['yc'] * lax.rsqrt(ctx['var'] + GN_EPS) * ln_g + ln_b
        ctx['out'] = (yn + ctx['bonus']) * ctx['g']

    return [mean, var, gate]


def _run(stages):
    for stage in stages:
        stage()


def _rwkv_prelude(ps, *params):
    ctx = {'ps': ps}
    _run(_prelude_stages(ctx, *params))
    return tuple(ctx[name] for name in ('r', 'k2', 'v', 'logw', 'a', 'g', 'kk'))


def _rwkv_postlude(y, r, k2, v, g, r_k, ln_g, ln_b, bones):
    ctx = {'y': y, 'bonus': _bonus(r, k2, v, r_k, bones), 'g': g}
    _run(_postlude_stages(ctx, ln_g, ln_b, bones))
    return ctx['out']


def _inproj_pool_kernel(x_ref, g1_ref, win_ref, pw_ref, ps_ref, mu_ref, wout_ref, wup_ref, wdn_ref,
                        sh_ref, po_ref, plast_ref, ulast_ref, wout_bf_ref, wup_bf_ref, wdn_bf_ref,
                        ext_ref, prow_ref):
    tb = pl.program_id(1)
    rows = x_ref.shape[0]

    wout_bf_ref[...] = wout_ref[...].astype(BF16)
    wup_bf_ref[...] = wup_ref[...].astype(BF16)
    wdn_bf_ref[...] = wdn_ref[...].astype(BF16)

    @pl.when(tb == 0)
    def _():
        ext_ref[0:POOL_HALO, :] = jnp.zeros((POOL_HALO, POOL_W), F32)
        prow_ref[...] = jnp.zeros_like(prow_ref)

    h = _rmsnorm(x_ref[...], g1_ref[...]).astype(BF16)
    proj = _dot(h, win_ref[...])
    u = proj[:, :POOL_W]
    ext_ref[POOL_HALO:POOL_HALO + rows, :] = u
    ulast_ref[...] = u[rows - POOL_HALO:, :]

    pf = proj[:, POOL_W:]
    first = lax.broadcasted_iota(jnp.int32, (rows, 1), 0) == 0
    prev = jnp.where(first, prow_ref[0:1, :], pltpu.roll(pf, 1, 0))
    sh_ref[...] = _token_shift(pf, prev, mu_ref[...])
    prow_ref[0:1, :] = pf[rows - 1:rows, :]
    plast_ref[...] = jnp.broadcast_to(pf[rows - 1:rows, :], plast_ref.shape)

    pos = tb * rows + lax.broadcasted_iota(jnp.int32, (rows, 1), 0)
    outs = []
    for gi, win in enumerate(POOL_WINDOWS):
        cols = slice(gi * POOL_GC, (gi + 1) * POOL_GC)
        acc = ext_ref[:, cols]
        shift = 1
        while shift < win:
            acc = acc + pltpu.roll(acc, shift, 0)
            shift *= 2
        acc = acc[POOL_HALO:, :]
        cnt = jnp.minimum(win, pos + 1).astype(F32)
        d = acc / cnt - u[:, cols]
        outs.append(_dot(d.astype(BF16), pw_ref[gi]))
    po_ref[...] = (jnp.concatenate(outs, axis=1) * ps_ref[...]).astype(po_ref.dtype)
    ext_ref[0:POOL_HALO, :] = ext_ref[rows:rows + POOL_HALO, :]


def _inproj_pool(x, g1, win, pw, ps, mu, later_weights):
    b, t, _ = x.shape
    rows = ROW_BLOCK
    nt = t // rows
    steps = b * nt
    const = lambda *shape: pl.BlockSpec(shape, lambda i, j: (0,) * len(shape), pipeline_mode=pl.Buffered(1))
    slab = lambda w: pl.BlockSpec((w.shape[0] // steps, w.shape[1]), lambda i, j: (i * nt + j, 0))
    return pl.pallas_call(
        _inproj_pool_kernel,
        grid=(b, nt),
        in_specs=[
            pl.BlockSpec((None, rows, D_MODEL), lambda i, j: (i, j, 0)),
            const(1, D_MODEL),
            const(D_MODEL, IN_W),
            const(len(POOL_WINDOWS), POOL_GC, POOL_GC),
            const(1, POOL_W),
            const(1, SHIFT_W),
        ] + [slab(w) for w in later_weights],
        out_specs=[
            pl.BlockSpec((None, rows, SHIFT_W), lambda i, j: (i, j, 0)),
            pl.BlockSpec((None, rows, POOL_W), lambda i, j: (i, j, 0)),
            pl.BlockSpec((None, 8, SHIFT_W), lambda i, j: (i, 0, 0)),
            pl.BlockSpec((None, POOL_HALO, POOL_W), lambda i, j: (i, 0, 0)),
        ] + [slab(w) for w in later_weights],
        out_shape=[
            jax.ShapeDtypeStruct((b, t, SHIFT_W), F32),
            jax.ShapeDtypeStruct((b, t, POOL_W), BF16),
            jax.ShapeDtypeStruct((b, 8, SHIFT_W), F32),
            jax.ShapeDtypeStruct((b, POOL_HALO, POOL_W), F32),
        ] + [jax.ShapeDtypeStruct(w.shape, BF16) for w in later_weights],
        scratch_shapes=[pltpu.VMEM((rows + POOL_HALO, POOL_W), F32), pltpu.VMEM((8, SHIFT_W), F32)],
        compiler_params=pltpu.CompilerParams(
            dimension_semantics=("arbitrary", "arbitrary"), vmem_limit_bytes=VMEM_LIMIT),
        name="inproj_pool",
    )(x, g1, win, pw, ps, mu, *later_weights)


def _expand(x_bf, half):
    zero = jnp.zeros((x_bf.shape[0], LANES), x_bf.dtype)
    blocks = []
    for h in range(GROUP_HEADS):
        tile = h // 2
        piece = x_bf[:, tile * LANES:(tile + 1) * LANES] * half[h % 2]
        blocks.append(jnp.concatenate([piece, zero] if tile == 0 else [zero, piece], axis=1))
    return jnp.concatenate(blocks, axis=0)


def _diag_blocks(full, low):
    tiles = []
    for tile in range(GROUP_HEADS // 2):
        cols = slice(tile * LANES, (tile + 1) * LANES)
        left = full[(2 * tile) * HEAD_DIM:(2 * tile + 1) * HEAD_DIM, cols]
        right = full[(2 * tile + 1) * HEAD_DIM:(2 * tile + 2) * HEAD_DIM, cols]
        tiles.append(jnp.where(low, left, right))
    return jnp.concatenate(tiles, axis=1)


def _chunk_stages(ctx, strict, incl, iden, half):
    c = CHUNK
    each = lambda fn, *names: [fn(*args) for args in zip(*(ctx[name] for name in names))]
    bf = lambda x: x.astype(BF16)
    blockdiag = lambda x: _expand(bf(x), half)
    rows2 = lambda x, y: jnp.concatenate([x, y], axis=0)

    def scores():
        ctx['ar'] = each(rows2, 'at', 'rt')
        e_bk = each(lambda b_, k_: rows2(_expand(b_, half), _expand(k_, half)), 'bt', 'kt')
        ctx['sc'] = [_dot_nt(a_, e_) for a_, e_ in zip(ctx['ar'], e_bk)]

    def masks():
        ctx['lm_k'] = each(lambda x: bf(rows2(x[:c, GROUP_W:] * strict, x[c:, GROUP_W:] * incl)), 'sc')
        ctx['m_rb'] = each(lambda x: bf(x[c:, :GROUP_W] * incl), 'sc')
        ctx['p'] = each(lambda x: x[:c, :GROUP_W] * strict, 'sc')
        ctx['n'] = each(lambda x: iden + x, 'p')
        del ctx['sc']

    def square():
        ctx['p'] = each(lambda x: _dot(bf(x), blockdiag(x)), 'p')

    def double():
        out = each(lambda p_, n_: _dot(bf(rows2(p_, n_)), blockdiag(p_)), 'p', 'n')
        ctx['p'] = [x[:c] for x in out]
        ctx['n'] = [n_ + x[c:] for n_, x in zip(ctx['n'], out)]

    def inverse():
        ctx['t_inv'] = each(lambda n_, p_: bf(n_ + _dot(bf(n_), blockdiag(p_))), 'n', 'p')

    def values():
        ctx['lv'] = each(lambda m_, v_: _dot(m_, _expand(v_, half)), 'lm_k', 'vb')

    def state_in():
        ctx['ps'] = each(lambda a_, s_: _dot_nt(a_, blockdiag(s_)), 'ar', 's0')

    def solve():
        ctx['ub'] = each(lambda t_, p_, l_: bf(_dot(t_, blockdiag(p_[:c] + l_[:c]))), 't_inv', 'ps', 'lv')

    def outputs():
        ctx['y'] = each(lambda p_, m_, u_, l_: p_[c:] + _dot(m_, _expand(u_, half)) + l_[c:], 'ps', 'm_rb', 'ub', 'lv')

    def state_out():
        low = lax.broadcasted_iota(jnp.int32, (HEAD_DIM, LANES), 1) < HEAD_DIM
        full = each(lambda u_, v_, b_, k_: lax.dot_general(rows2(u_, v_), rows2(b_, k_), (((0,), (0,)), ((), ())),
                                                           preferred_element_type=F32), 'ub', 'vb', 'bt', 'kt')
        ctx['s_new'] = [(s_ + _diag_blocks(f_, low)) * w_ for s_, f_, w_ in zip(ctx['s0'], full, ctx['w_last'])]

    levels = int(np.log2(c))
    return ([scores, masks, square] + [double] * (levels - 2)
            + [inverse, values, state_in, solve, outputs, state_out])


def _rwkv_chunk_kernel(sh_ref, w0_ref, a0_ref, kk_ref, ka_ref, rk_ref, lng_ref, lnb_ref,
                       w2_ref, glup_ref, bones_ref, strict_ref, incl_ref, iden_ref, half_ref, tri_ref,
                       rw_ref, sfin_ref, s_scr):
    t = pl.program_id(1)
    nb, c, _ = sh_ref.shape
    seqs = range(nb)

    @pl.when(t == 0)
    def _():
        s_scr[...] = jnp.zeros_like(s_scr)

    bones = bones_ref[...]
    pre_params = (w0_ref[...], a0_ref[...], kk_ref[...], ka_ref[...], w2_ref[...], glup_ref[...], bones)
    chain_consts = (strict_ref[...], incl_ref[...], iden_ref[...], half_ref[...])
    tri = tri_ref[...]
    groups = [slice(gi * GROUP_W, (gi + 1) * GROUP_W) for gi in range(N_GROUPS)]
    rows = [slice(n * c, (n + 1) * c) for n in seqs]
    ctx = {'ps': jnp.concatenate([sh_ref[n] for n in seqs], axis=0)}

    def cumulative_decay():
        logw = ctx['logw']
        hi = logw.astype(BF16)
        lo = (logw - hi.astype(F32)).astype(BF16)
        ctx['cl'] = [_dot(tri, jnp.concatenate([hi[rs], lo[rs]], axis=0)) for rs in rows]

    def scale():
        aa = -ctx['kk']
        bb = ctx['kk'] * ctx['a']
        chain = {name: [] for name in ('at', 'rt', 'bt', 'kt', 'vb', 'w_last')}
        for rs, cl in zip(rows, ctx['cl']):
            w_in = jnp.exp2(cl)
            w_ex = jnp.exp2(cl - ctx['logw'][rs])
            w_inv = jnp.exp2(-cl)
            for sl in groups:
                chain['at'].append((aa[rs, sl] * w_ex[:, sl]).astype(BF16))
                chain['rt'].append((ctx['r'][rs, sl] * w_in[:, sl]).astype(BF16))
                chain['bt'].append((bb[rs, sl] * w_inv[:, sl]).astype(BF16))
                chain['kt'].append((ctx['k2'][rs, sl] * w_inv[:, sl]).astype(BF16))
                chain['vb'].append(ctx['v'][rs, sl].astype(BF16))
                chain['w_last'].append(w_in[c - 1:c, sl])
        ctx.update(chain)
        ctx['s0'] = [s_scr[n, :, sl] for n in seqs for sl in groups]

    def store_state():
        for n in seqs:
            for gi, sl in enumerate(groups):
                s_scr[n, :, sl] = ctx['s_new'][n * N_GROUPS + gi]
        ctx['y'] = jnp.concatenate([jnp.concatenate(ctx['y'][n * N_GROUPS:(n + 1) * N_GROUPS], axis=1)
                                    for n in seqs], axis=0)

    def bonus():
        ctx['bonus'] = _bonus(ctx['r'], ctx['k2'], ctx['v'], rk_ref[...], bones)

    def store_out():
        rw_ref[...] = ctx['out'].reshape(nb, c, RWKV_W).astype(rw_ref.dtype)

    _run(_prelude_stages(ctx, *pre_params) + [cumulative_decay, scale]
         + _chunk_stages(ctx, *chain_consts) + [store_state, bonus]
         + _postlude_stages(ctx, lng_ref[...], lnb_ref[...], bones) + [store_out])

    @pl.when(t == pl.num_programs(1) - 1)
    def _():
        for h in range(N_HEADS):
            sfin_ref[:, h] = s_scr[:, :, h * HEAD_DIM:(h + 1) * HEAD_DIM]


def _chunk_constants():
    c = CHUNK
    t = np.arange(c)[:, None]
    s = (np.arange(GROUP_HEADS * c) % c)[None, :]
    strict = (s < t).astype(np.float32)
    incl = (s <= t).astype(np.float32)
    iden = (s == t).astype(np.float32)
    lane_low = np.broadcast_to(np.arange(LANES)[None, :] < HEAD_DIM, (c, LANES))
    half = np.stack([lane_low, ~lane_low]).astype(np.float32)
    tri = (np.arange(c)[None, :] <= np.arange(c)[:, None]).astype(np.float32)
    tri = np.concatenate([tri, tri], axis=1)
    return (jnp.asarray(strict), jnp.asarray(incl), jnp.asarray(iden),
            jnp.asarray(half, dtype=BF16), jnp.asarray(tri, dtype=BF16))


def _head_ones():
    h = np.arange(GROUP_W) // HEAD_DIM
    return jnp.asarray((h[:, None] == h[None, :]).astype(np.float32), dtype=BF16)


def _rwkv_chunked(p, rows_prm, w2, gl_up, bones):
    b, t, _ = p.shape
    c = CHUNK
    nb = SEQ_BLOCK
    consts = _chunk_constants()
    const2 = lambda arr: pl.BlockSpec(arr.shape, lambda i, j: (0,) * arr.ndim)
    operands = list(rows_prm) + [w2, gl_up, bones] + list(consts)
    return pl.pallas_call(
        _rwkv_chunk_kernel,
        grid=(b // nb, t // c),
        in_specs=[pl.BlockSpec((nb, c, SHIFT_W), lambda i, j: (i, j, 0))] + [const2(a) for a in operands],
        out_specs=[
            pl.BlockSpec((nb, c, RWKV_W), lambda i, j: (i, j, 0)),
            pl.BlockSpec((nb, N_HEADS, HEAD_DIM, HEAD_DIM), lambda i, j: (i, 0, 0, 0)),
        ],
        out_shape=[
            jax.ShapeDtypeStruct((b, t, RWKV_W), BF16),
            jax.ShapeDtypeStruct((b, N_HEADS, HEAD_DIM, HEAD_DIM), F32),
        ],
        scratch_shapes=[pltpu.VMEM((nb, HEAD_DIM, RWKV_W), F32)],
        compiler_params=pltpu.CompilerParams(
            dimension_semantics=("arbitrary", "arbitrary"), vmem_limit_bytes=VMEM_LIMIT),
        name="rwkv_chunked",
    )(p, *operands)


def _outproj_mlp_rows(x, po, rw, wout_ref, g2, wup_ref, wdn_ref, gf):
    mix = _dot(po, wout_ref[0:POOL_W, :]) + _dot(rw, wout_ref[POOL_W:, :])
    x1 = x + mix
    h2 = _rmsnorm(x1, g2).astype(BF16)
    acc = jnp.zeros_like(x1)
    for j in range(D_FF // FF_BLOCK):
        cols = slice(j * FF_BLOCK, (j + 1) * FF_BLOCK)
        hid = jnp.maximum(_dot(h2, wup_ref[:, cols]), 0.0)
        acc = acc + _dot((hid * hid).astype(BF16), wdn_ref[cols, :])
    return _rmsnorm(x1 + acc, gf)


def _outproj_mlp_kernel(x_ref, po_ref, rw_ref, xs_ref, pos_ref, rws_ref, wout_ref, g2_ref, wup_ref, wdn_ref, gf_ref,
                        y_ref, ys_ref):
    step = pl.program_id(0)
    last = pl.num_programs(0) - 1
    weights = (wout_ref, g2_ref[...], wup_ref, wdn_ref, gf_ref[...])

    @pl.when(step < last)
    def _():
        y_ref[...] = _outproj_mlp_rows(x_ref[...], po_ref[...], rw_ref[...], *weights)

    @pl.when(step == last)
    def _():
        ys_ref[...] = _outproj_mlp_rows(xs_ref[...], pos_ref[...], rws_ref[...], *weights)


def _outproj_mlp(x, po, rw, xs, pos, rws, wout, g2, wup, wdn, gf):
    n, ns = x.shape[0], xs.shape[0]
    rows = MLP_ROW_BLOCK
    blocks = n // rows
    const = lambda *shape: pl.BlockSpec(shape, lambda i: (0,) * len(shape), pipeline_mode=pl.Buffered(1))
    prompt = lambda width: pl.BlockSpec((rows, width), lambda i: (jnp.minimum(i, blocks - 1), 0))
    sample = lambda width: pl.BlockSpec((ns, width), lambda i: (0, 0))
    return pl.pallas_call(
        _outproj_mlp_kernel,
        grid=(blocks + 1,),
        in_specs=[
            prompt(D_MODEL), prompt(POOL_W), prompt(RWKV_W),
            sample(D_MODEL), sample(POOL_W), sample(RWKV_W),
            const(D_MODEL, D_MODEL),
            const(1, D_MODEL),
            const(D_MODEL, D_FF),
            const(D_FF, D_MODEL),
            const(1, D_MODEL),
        ],
        out_specs=[prompt(D_MODEL), sample(D_MODEL)],
        out_shape=[jax.ShapeDtypeStruct((n, D_MODEL), F32), jax.ShapeDtypeStruct((ns, D_MODEL), F32)],
        compiler_params=pltpu.CompilerParams(
            dimension_semantics=("arbitrary",), vmem_limit_bytes=VMEM_LIMIT),
        name="outproj_mlp",
    )(x, po, rw, xs, pos, rws, wout, g2, wup, wdn, gf)


def _sample_front_kernel(x_ref, sp_ref, sh_ref, g1_ref, win_ref, pw_ref, ps_ref, mu_ref, w0_ref, a0_ref,
                         kk_ref, ka_ref, w2_ref, glup_ref, bones_ref,
                         u_ref, p_ref, po_ref, tvec_ref, vec_ref):
    h = _rmsnorm(x_ref[...], g1_ref[...]).astype(BF16)
    proj = _dot(h, win_ref[...])
    u = proj[:, :POOL_W]
    pf = proj[:, POOL_W:]
    u_ref[...] = u
    p_ref[...] = pf

    outs = []
    for gi, win in enumerate(POOL_WINDOWS):
        cols = slice(gi * POOL_GC, (gi + 1) * POOL_GC)
        acc = u[:, cols]
        for j in range(1, win):
            acc = acc + sp_ref[POOL_BUF - j, :, cols]
        d = acc * (1.0 / win) - u[:, cols]
        outs.append(_dot(d.astype(BF16), pw_ref[gi]))
    po_ref[...] = (jnp.concatenate(outs, axis=1) * ps_ref[...]).astype(po_ref.dtype)

    bones = bones_ref[...]
    r, k2, v, logw, a, g, kk = _rwkv_prelude(
        _token_shift(pf, sh_ref[...], mu_ref[...]), w0_ref[...], a0_ref[...], kk_ref[...], ka_ref[...],
        w2_ref[...], glup_ref[...], bones)
    w = jnp.exp2(logw)
    aa = -kk
    bb = kk * a
    q = w * r + aa * _head_sum(bb * r, bones)
    yv = v * _head_sum(k2 * r, bones)
    for i, x in enumerate((aa, q, w, bb, k2, v)):
        tvec_ref[i] = x.T
    for i, x in enumerate((yv, r, k2, v, g)):
        vec_ref[i] = x


def _sample_front(x, sp, sh, g1, win, pw, ps, rows_prm, w2, gl_up, bones):
    n = x.shape[0]
    mu, w0, a0, k_k, k_a = rows_prm
    return pl.pallas_call(
        _sample_front_kernel,
        out_shape=[
            jax.ShapeDtypeStruct((n, POOL_W), F32),
            jax.ShapeDtypeStruct((n, SHIFT_W), F32),
            jax.ShapeDtypeStruct((n, POOL_W), BF16),
            jax.ShapeDtypeStruct((6, RWKV_W, n), F32),
            jax.ShapeDtypeStruct((5, n, RWKV_W), F32),
        ],
        compiler_params=pltpu.CompilerParams(vmem_limit_bytes=VMEM_LIMIT),
        name="sample_front",
    )(x, sp, sh, g1, win, pw, ps, mu, w0, a0, k_k, k_a, w2, gl_up, bones)


def _decode_state_kernel(s_ref, tvec_ref, snew_ref, yt_ref):
    rows = s_ref.shape[0]
    v0 = pl.program_id(1) * rows
    aa = tvec_ref[0]
    q = tvec_ref[1]
    w = tvec_ref[2]
    bb = tvec_ref[3]
    k = tvec_ref[4]
    for i in range(rows):
        s0 = s_ref[i]
        v_row = tvec_ref[5, pl.ds(v0 + i, 1), :]
        sa = jnp.sum(s0 * aa, axis=0, keepdims=True)
        yt_ref[i:i + 1, :] = jnp.sum(s0 * q, axis=0, keepdims=True)
        snew_ref[i] = s0 * w + sa * bb + v_row * k


def _decode_state(s, tvec):
    n = s.shape[-1]
    rows = DEC_ROWS
    return pl.pallas_call(
        _decode_state_kernel,
        grid=(N_HEADS, HEAD_DIM // rows),
        in_specs=[
            pl.BlockSpec((None, rows, HEAD_DIM, n), lambda h, j: (h, j, 0, 0)),
            pl.BlockSpec((6, None, HEAD_DIM, n), lambda h, j: (0, h, 0, 0)),
        ],
        out_specs=[
            pl.BlockSpec((None, rows, HEAD_DIM, n), lambda h, j: (h, j, 0, 0)),
            pl.BlockSpec((None, rows, n), lambda h, j: (h, j, 0)),
        ],
        out_shape=[
            jax.ShapeDtypeStruct(s.shape, F32),
            jax.ShapeDtypeStruct((N_HEADS, HEAD_DIM, n), F32),
        ],
        compiler_params=pltpu.CompilerParams(
            dimension_semantics=("arbitrary", "arbitrary"), vmem_limit_bytes=VMEM_LIMIT),
        name="decode_state",
    )(s, tvec)


def _sample_post_kernel(y0t_ref, vec_ref, rk_ref, lng_ref, lnb_ref, bones_ref, rw_ref):
    y = y0t_ref[...].T + vec_ref[0]
    out = _rwkv_postlude(y, vec_ref[1], vec_ref[2], vec_ref[3], vec_ref[4],
                         rk_ref[...], lng_ref[...], lnb_ref[...], bones_ref[...])
    rw_ref[...] = out.astype(rw_ref.dtype)


def _sample_post(y0t, vec, r_k, ln_g, ln_b, bones):
    n = y0t.shape[1]
    return pl.pallas_call(
        _sample_post_kernel,
        out_shape=jax.ShapeDtypeStruct((n, RWKV_W), BF16),
        compiler_params=pltpu.CompilerParams(vmem_limit_bytes=VMEM_LIMIT),
        name="sample_post",
    )(y0t, vec, r_k, ln_g, ln_b, bones)


def kernel(x_prompt, x_sample, state_wkv, state_shift, state_pool, norm1_g, w_in, shift_mu, pool_w, pool_scale,
           w0, w_lora_up, a0, a_lora_up, g_lora_up, k_k, k_a, r_k, ln_x_g, ln_x_b, w_out, norm2_g, w_up, w_down,
           norm_f_g):
    depth = w_in.shape[0]
    assert depth == 1, "single-layer step"
    b, t, _ = x_prompt.shape
    n = x_sample.shape[0]
    row = lambda a: a.reshape(1, -1).astype(F32)

    g1 = row(norm1_g[0])
    win = w_in[0].astype(BF16)
    pw = pool_w[0].astype(BF16)
    ps = row(pool_scale[0])
    zeros = jnp.zeros((64, RWKV_W), F32)
    w2 = jnp.concatenate([jnp.concatenate([w_lora_up[0], zeros], axis=1),
                          jnp.concatenate([zeros, a_lora_up[0]], axis=1)], axis=0).astype(BF16)
    gl_up = g_lora_up[0].astype(BF16)
    bones = _head_ones()
    mu = row(shift_mu[0])
    pre_rows = (row(w0[0]), row(a0[0]), row(k_k[0]), jnp.concatenate([row(k_a[0]), 1.0 - row(k_a[0])], axis=0))
    post_rows = (row(r_k[0]), row(ln_x_g[0]), row(ln_x_b[0]))
    g2 = row(norm2_g[0])
    gf = row(norm_f_g)

    sh_p, po_p, plast, ulast, wout, wup, wdn = _inproj_pool(x_prompt, g1, win, pw, ps, mu,
                                                            (w_out[0], w_up[0], w_down[0]))
    rw_p, s_fin = _rwkv_chunked(sh_p, pre_rows + post_rows, w2, gl_up, bones)
    wkv_p = s_fin[None]
    shift_p = plast[:, 0:1, :][None]
    pool_p = ulast[:, POOL_HALO - POOL_BUF:, :][None]

    xs = x_sample.reshape(n, D_MODEL)
    sp = state_pool[0]
    u_s, p_s, po_s, tvec, vec = _sample_front(xs, jnp.transpose(sp, (1, 0, 2)), state_shift[0].reshape(n, SHIFT_W),
                                              g1, win, pw, ps, (mu,) + pre_rows, w2, gl_up, bones)
    s_t = jnp.transpose(state_wkv[0], (1, 2, 3, 0))
    s_new_t, y0t = _decode_state(s_t, tvec.reshape(6, N_HEADS, HEAD_DIM, n))
    s_new = jnp.transpose(s_new_t, (3, 0, 1, 2))
    rw_s = _sample_post(y0t.reshape(RWKV_W, n), vec, *post_rows, bones)
    shift_s = p_s.reshape(1, n, 1, SHIFT_W)
    pool_s = jnp.concatenate([sp[:, 1:, :], u_s[:, None, :]], axis=1)[None]

    y_p, y_s = _outproj_mlp(x_prompt.reshape(b * t, D_MODEL), po_p.reshape(b * t, POOL_W),
                            rw_p.reshape(b * t, RWKV_W), xs, po_s, rw_s, wout, g2, wup, wdn, gf)

    return (y_p.reshape(b, t, D_MODEL), y_s.reshape(n, 1, D_MODEL), wkv_p, shift_p, pool_p, s_new[None],
            shift_s, pool_s)
```
